```python
import jax, jax.numpy as jnp
from jax import lax
import numpy as np

D_MODEL = 1024
BATCH = 32
SEQ = 2048
DEPTH = 4

CHUNK = 128
GMLP_WIDTH = D_MODEL // 2
GMLP_HEADS = 4
GMLP_HEAD_DIM = GMLP_WIDTH // GMLP_HEADS
RET_WIDTH = D_MODEL // 2
RET_HEADS = 4
RET_HEAD_DIM = RET_WIDTH // RET_HEADS
MIX_WIDTH = GMLP_WIDTH + RET_WIDTH
IN_COLS = 2 * GMLP_WIDTH + 4 * RET_WIDTH
ROPE_BASE = 10000.0
N_EXPERTS = 32
TOP_K = 4
EXPERT_FF = D_MODEL
SWIGLU_ALPHA = 1.702
SWIGLU_LIMIT = 7.0
BLOCK = 128
EPS = 1e-5
N_MOD = 6

kernel_name = "hybrid_gmlp_retention_moe_adaln"

F32 = jnp.float32


def rms_norm(x, g):
    xf = x.astype(F32)
    y = xf * lax.rsqrt(jnp.mean(xf * xf, axis=-1, keepdims=True) + EPS)
    return (y * g.astype(F32)).astype(x.dtype)


def layer_norm(x, g, b):
    xf = x.astype(F32)
    xc = xf - jnp.mean(xf, axis=-1, keepdims=True)
    var = jnp.mean(xc * xc, axis=-1, keepdims=True)
    return (xc * lax.rsqrt(var + EPS) * g.astype(F32) + b.astype(F32)).astype(x.dtype)


def rotary(x):
    seq, d = x.shape[1], x.shape[-1]
    half = d // 2
    inv_freq = ROPE_BASE ** (-jnp.arange(half, dtype=F32) / half)
    ang = jnp.arange(seq, dtype=F32)[:, None] * inv_freq[None, :]
    cos = jnp.cos(ang)[None, :, None, :]
    sin = jnp.sin(ang)[None, :, None, :]
    x1, x2 = x[..., :half], x[..., half:]
    return jnp.concatenate([x1 * cos - x2 * sin, x1 * sin + x2 * cos], axis=-1)


def chunked_spatial_gating(u, v, w_s, b_s, ln_g, ln_b):
    bsz, seq, _ = v.shape
    nc = seq // CHUNK
    v = layer_norm(v, ln_g, ln_b)
    vh = v.reshape(bsz, nc, CHUNK, GMLP_HEADS, GMLP_HEAD_DIM)
    causal = jnp.tril(jnp.ones((CHUNK, CHUNK), dtype=bool))
    w = jnp.where(causal[None], w_s, jnp.zeros_like(w_s))
    mixed = jnp.einsum('hts,bnshd->bnthd', w, vh) + b_s.T[None, None, :, :, None]
    return u * mixed.reshape(bsz, seq, GMLP_WIDTH)


def chunkwise_retention(q, k, v):
    bsz, seq, n_heads, d = q.shape
    nc = seq // CHUNK
    gamma = 1.0 - jnp.exp2(-5.0 - jnp.arange(n_heads, dtype=F32))
    log_g = jnp.log(gamma)
    idx = jnp.arange(CHUNK, dtype=F32)
    diff = idx[:, None] - idx[None, :]
    intra_decay = jnp.where(diff[None] >= 0,
                            jnp.exp(log_g[:, None, None] * jnp.maximum(diff, 0.0)[None]),
                            0.0)
    qc = q.reshape(bsz, nc, CHUNK, n_heads, d)
    kc = k.reshape(bsz, nc, CHUNK, n_heads, d)
    vc = v.reshape(bsz, nc, CHUNK, n_heads, d)
    scores = jnp.einsum('bnthk,bnshk->bnhts', qc, kc) * intra_decay[None, None]
    intra = jnp.einsum('bnhts,bnshv->bnthv', scores, vc)
    k_decay = jnp.exp(log_g[:, None] * (CHUNK - 1.0 - idx)[None, :])
    chunk_kv = jnp.einsum('bnshk,hs,bnshv->nbhkv', kc, k_decay, vc)
    chunk_decay = jnp.exp(log_g * CHUNK)

    def step(state, kv):
        return state * chunk_decay[None, :, None, None] + kv, state

    _, prev_state = lax.scan(step, jnp.zeros((bsz, n_heads, d, d), F32), chunk_kv)
    q_decay = jnp.exp(log_g[:, None] * (idx + 1.0)[None, :])
    cross = jnp.einsum('bnthk,nbhkv,ht->bnthv', qc, prev_state, q_decay)
    return (intra + cross).reshape(bsz, seq, n_heads, d)


def hybrid_mixer(h, w_in, ln_g, ln_b, w_s, b_s, ret_norm_g, w_out):
    bsz, seq, _ = h.shape
    proj = jnp.dot(h, w_in)
    gw, rw = GMLP_WIDTH, RET_WIDTH
    u, v, q, k, vr, g = jnp.split(
        proj, [gw, 2 * gw, 2 * gw + rw, 2 * gw + 2 * rw, 2 * gw + 3 * rw], axis=-1)
    out_a = chunked_spatial_gating(jax.nn.gelu(u), jax.nn.gelu(v), w_s, b_s, ln_g, ln_b)
    shp = (bsz, seq, RET_HEADS, RET_HEAD_DIM)
    qf = rotary(q.astype(F32).reshape(shp))
    kf = rotary(k.astype(F32).reshape(shp)) * (RET_HEAD_DIM ** -0.5)
    ret = chunkwise_retention(qf, kf, vr.astype(F32).reshape(shp))
    ret = ret * lax.rsqrt(jnp.mean(ret * ret, axis=-1, keepdims=True) + EPS)
    ret = ret * ret_norm_g.astype(F32).reshape(RET_HEADS, RET_HEAD_DIM)
    out_b = ret.reshape(bsz, seq, rw).astype(h.dtype) * jax.nn.silu(g)
    return jnp.dot(jnp.concatenate([out_a, out_b], axis=-1), w_out)


def clamped_swiglu(a):
    a_glu, a_lin = jnp.split(a, 2, axis=-1)
    a_glu = jnp.minimum(a_glu, SWIGLU_LIMIT)
    a_lin = jnp.clip(a_lin, -SWIGLU_LIMIT, SWIGLU_LIMIT)
    return a_glu * jax.nn.sigmoid(SWIGLU_ALPHA * a_glu) * (a_lin + 1.0)


def moe_ffn(h, w_router, b_router, w1, b1, w2, b2):
    bsz, seq, d = h.shape
    t = h.reshape(-1, d)
    n_tok = t.shape[0]
    n_assign = n_tok * TOP_K
    logits = jnp.dot(t, w_router).astype(F32) + b_router.astype(F32)
    top_logits, top_idx = lax.top_k(logits, TOP_K)
    top_w = jax.nn.softmax(top_logits, axis=-1)
    flat_e = top_idx.reshape(-1)
    flat_tok = jnp.arange(n_assign, dtype=jnp.int32) // TOP_K
    flat_w = top_w.reshape(-1)
    order = jnp.argsort(flat_e, stable=True)
    se, stok, sw = flat_e[order], flat_tok[order], flat_w[order]
    counts = jnp.bincount(flat_e, length=N_EXPERTS)
    group_start = jnp.cumsum(counts) - counts
    padded = ((counts + BLOCK - 1) // BLOCK) * BLOCK
    padded_end = jnp.cumsum(padded)
    padded_start = padded_end - padded
    slot = padded_start[se] + (jnp.arange(n_assign, dtype=jnp.int32) - group_start[se])
    n_slots = n_assign + N_EXPERTS * BLOCK
    n_blocks = n_slots // BLOCK
    slot_tok = jnp.zeros((n_slots,), jnp.int32).at[slot].set(stok)
    slot_w = jnp.zeros((n_slots,), F32).at[slot].set(sw)
    block_expert = jnp.minimum(
        jnp.searchsorted(padded_end, jnp.arange(n_blocks) * BLOCK, side='right'), N_EXPERTS - 1)

    def run_block(args):
        tok, e = args
        a = jnp.dot(t[tok], w1[e]) + b1[e]
        return jnp.dot(clamped_swiglu(a), w2[e]) + b2[e]

    y = lax.map(run_block, (slot_tok.reshape(n_blocks, BLOCK), block_expert))
    y = y.reshape(n_slots, d) * slot_w[:, None].astype(y.dtype)
    out = jnp.zeros_like(t).at[slot_tok].add(y)
    return out.reshape(bsz, seq, d)


def setup_inputs(seed: int = 0) -> dict:
    key = jax.random.key(seed)
    ks = jax.random.split(key, 20)
    d, f = D_MODEL, EXPERT_FF
    nrm = lambda k, shp, s: jax.random.normal(k, shp, F32) * s
    return {
        "x": nrm(ks[0], (BATCH, SEQ, d), 1.0),
        "c": nrm(ks[1], (BATCH, d), 1.0),
        "w_ada": nrm(ks[2], (DEPTH, d, N_MOD * d), 0.5 * d ** -0.5),
        "b_ada": nrm(ks[3], (DEPTH, N_MOD * d), 0.01),
        "g_mix": 1.0 + nrm(ks[4], (DEPTH, d), 0.01),
        "w_in": nrm(ks[5], (DEPTH, d, IN_COLS), d ** -0.5),
        "gmlp_ln_g": 1.0 + nrm(ks[6], (DEPTH, GMLP_WIDTH), 0.01),
        "gmlp_ln_b": nrm(ks[7], (DEPTH, GMLP_WIDTH), 0.01),
        "w_spatial": nrm(ks[8], (DEPTH, GMLP_HEADS, CHUNK, CHUNK), CHUNK ** -0.5),
        "b_spatial": 1.0 + nrm(ks[9], (DEPTH, GMLP_HEADS, CHUNK), 0.01),
        "ret_norm_g": 1.0 + nrm(ks[10], (DEPTH, RET_WIDTH), 0.01),
        "w_out": nrm(ks[11], (DEPTH, MIX_WIDTH, d), MIX_WIDTH ** -0.5),
        "g_ffn": 1.0 + nrm(ks[12], (DEPTH, d), 0.01),
        "w_router": nrm(ks[13], (DEPTH, d, N_EXPERTS), d ** -0.5),
        "b_router": nrm(ks[14], (DEPTH, N_EXPERTS), 0.01),
        "w1": nrm(ks[15], (DEPTH, N_EXPERTS, d, 2 * f), d ** -0.5),
        "b1": nrm(ks[16], (DEPTH, N_EXPERTS, 2 * f), 0.01),
        "w2": nrm(ks[17], (DEPTH, N_EXPERTS, f, d), f ** -0.5),
        "b2": nrm(ks[18], (DEPTH, N_EXPERTS, d), 0.01),
        "g_final": 1.0 + nrm(ks[19], (d,), 0.01),
    }


def reference(x, c, w_ada, b_ada, g_mix, w_in, gmlp_ln_g, gmlp_ln_b, w_spatial, b_spatial,
              ret_norm_g, w_out, g_ffn, w_router, b_router, w1, b1, w2, b2, g_final):
    c_act = jax.nn.silu(c)
    for l in range(DEPTH):
        mod = jnp.dot(c_act, w_ada[l]) + b_ada[l]
        sh_m, sc_m, gt_m, sh_f, sc_f, gt_f = [m[:, None, :] for m in jnp.split(mod, N_MOD, axis=-1)]
        h = rms_norm(x, g_mix[l]) * (1.0 + sc_m) + sh_m
        x = x + gt_m * hybrid_mixer(h, w_in[l], gmlp_ln_g[l], gmlp_ln_b[l], w_spatial[l],
                                    b_spatial[l], ret_norm_g[l], w_out[l])
        h = rms_norm(x, g_ffn[l]) * (1.0 + sc_f) + sh_f
        x = x + gt_f * moe_ffn(h, w_router[l], b_router[l], w1[l], b1[l], w2[l], b2[l])
    return rms_norm(x, g_final)
```

```python
import functools

import jax
import jax.numpy as jnp
from jax import lax
from jax.experimental import pallas as pl
from jax.experimental.pallas import tpu as pltpu

F32 = jnp.float32
BF16 = jnp.bfloat16

D_MODEL = 1024
CHUNK = 128
N_HEADS = 4
HEAD_DIM = 128
GROUP_W = N_HEADS * HEAD_DIM
N_EXPERTS = 32
TOP_K = 4
EXPERT_FF = 1024
N_MOD = 6
EPS = 1e-5
ROPE_BASE = 10000.0
SWIGLU_ALPHA = 1.702
SWIGLU_LIMIT = 7.0

TM = 256
BM = 256
TD = 256
VMEM_LIMIT = 56 * 1024 * 1024


def _gelu_tanh(x):
    return x * (0.5 * (1.0 + jnp.tanh(0.7978845608028654 * (x + 0.044715 * (x * x * x)))))


def _silu(x):
    return x * jax.nn.sigmoid(x)


def _ada_kernel(c_ref, w_ref, b_ref, o_ref):
    ca = _silu(c_ref[...]).astype(BF16)
    o_ref[0] = jnp.dot(ca, w_ref[0].astype(BF16), preferred_element_type=F32) + b_ref[0]


def _ada_call(c, w_ada, b_ada):
    depth, d, _ = w_ada.shape
    bsz = c.shape[0]
    return pl.pallas_call(
        _ada_kernel,
        grid=(depth, N_MOD),
        in_specs=[
            pl.BlockSpec((bsz, d), lambda l, n: (0, 0)),
            pl.BlockSpec((1, d, d), lambda l, n: (l, 0, n)),
            pl.BlockSpec((1, 1, d), lambda l, n: (l, 0, n)),
        ],
        out_specs=pl.BlockSpec((1, bsz, d), lambda l, n: (l, 0, n)),
        out_shape=jax.ShapeDtypeStruct((depth, bsz, N_MOD * d), F32),
        name="adaln_mod",
    )(c, w_ada, b_ada.reshape(depth, 1, N_MOD * d))


def _mixer_kernel(x_ref, shm_ref, scm_ref, gtm_ref, shf_ref, scf_ref, gmix_ref, gffn_ref,
                  win_ref, lng_ref, lnb_ref, ws_ref, bs_ref, rng_ref, wout_ref,
                  cos_ref, sin_ref, dmat_ref, qdec_ref, kdec_ref, cdec_ref, wr_ref, br_ref,
                  x1_ref, h2_ref, idx_ref, wgt_ref, rank_ref, cnt_ref,
                  state_ref, base_ref, ob_ref):
    b = pl.program_id(0)
    j = pl.program_id(1)
    n_chunks = TM // CHUNK

    @pl.when(j == 0)
    def _():
        state_ref[...] = jnp.zeros_like(state_ref)

    @pl.when((b == 0) & (j == 0))
    def _():
        base_ref[...] = jnp.zeros_like(base_ref)

    x = x_ref[0]
    ms = jnp.mean(x * x, axis=-1, keepdims=True)
    h = x * lax.rsqrt(ms + EPS) * (gmix_ref[0] * (1.0 + scm_ref[0])) + shm_ref[0]
    hb = h.astype(BF16)

    def proj(k):
        return jnp.dot(hb, win_ref[0, :, k * GROUP_W:(k + 1) * GROUP_W],
                       preferred_element_type=F32)

    def rows(c):
        return slice(c * CHUNK, (c + 1) * CHUNK)

    def lanes(hh):
        return slice(hh * HEAD_DIM, (hh + 1) * HEAD_DIM)

    gu = _gelu_tanh(proj(0))
    gv = _gelu_tanh(proj(1))
    mu = jnp.mean(gv, axis=-1, keepdims=True)
    vc = gv - mu
    var = jnp.mean(vc * vc, axis=-1, keepdims=True)
    vn = (vc * lax.rsqrt(var + EPS) * lng_ref[0] + lnb_ref[0]).astype(BF16)
    t_io = lax.broadcasted_iota(jnp.int32, (CHUNK, CHUNK), 0)
    s_io = lax.broadcasted_iota(jnp.int32, (CHUNK, CHUNK), 1)
    causal = t_io >= s_io
    bias = bs_ref[0]
    for hh in range(N_HEADS):
        wm = jnp.where(causal, ws_ref[0, hh], 0.0).astype(BF16)
        for c in range(n_chunks):
            mixed = jnp.dot(wm, vn[rows(c), lanes(hh)], preferred_element_type=F32)
            oa = gu[rows(c), lanes(hh)] * (mixed + bias[:, lanes(hh)])
            ob_ref[rows(c), lanes(hh)] = oa.astype(BF16)

    q = proj(2)
    k = proj(3)
    vr = proj(4).astype(BF16)
    g = proj(5)
    cos = cos_ref[...]
    sin = sin_ref[...]
    k_scale = HEAD_DIM ** -0.5
    for hh in range(N_HEADS):
        qh = q[:, lanes(hh)]
        kh = k[:, lanes(hh)]
        qh = qh * cos + pltpu.roll(qh, HEAD_DIM // 2, 1) * sin
        kh = (kh * cos + pltpu.roll(kh, HEAD_DIM // 2, 1) * sin) * k_scale
        dm = dmat_ref[hh]
        qd = qdec_ref[:, lanes(hh)]
        kd = kdec_ref[:, lanes(hh)]
        cd = cdec_ref[:, lanes(hh)]
        gn = rng_ref[0][:, lanes(hh)]
        for c in range(n_chunks):
            qc = qh[rows(c)]
            kc = kh[rows(c)]
            vcb = vr[rows(c), lanes(hh)]
            s = lax.dot_general(qc.astype(BF16), kc.astype(BF16),
                                (((1,), (1,)), ((), ())), preferred_element_type=F32) * dm
            st = state_ref[hh]
            o = jnp.dot(s.astype(BF16), vcb, preferred_element_type=F32)
            o = o + jnp.dot((qc * qd).astype(BF16), st.astype(BF16), preferred_element_type=F32)
            kdt = (kc * kd).T.astype(BF16)
            state_ref[hh] = st * cd + jnp.dot(kdt, vcb, preferred_element_type=F32)
            o = o * lax.rsqrt(jnp.mean(o * o, axis=-1, keepdims=True) + EPS) * gn
            ob = o * _silu(g[rows(c), lanes(hh)])
            ob_ref[rows(c), GROUP_W + hh * HEAD_DIM:GROUP_W + (hh + 1) * HEAD_DIM] = ob.astype(BF16)

    y = jnp.dot(ob_ref[...], wout_ref[0], preferred_element_type=F32)
    x1 = x + gtm_ref[0] * y
    x1_ref[0] = x1

    ms2 = jnp.mean(x1 * x1, axis=-1, keepdims=True)
    h2 = x1 * lax.rsqrt(ms2 + EPS) * (gffn_ref[0] * (1.0 + scf_ref[0])) + shf_ref[0]
    h2_ref[...] = h2

    lg = lax.dot_general(wr_ref[0], h2.astype(BF16), (((1,), (1,)), ((), ())),
                         preferred_element_type=F32) + br_ref[0]
    e_io = lax.broadcasted_iota(jnp.int32, (N_EXPERTS, TM), 0)
    work = lg
    vals, ids = [], []
    for _ in range(TOP_K):
        m = jnp.max(work, axis=0, keepdims=True)
        ik = jnp.min(jnp.where(work == m, e_io, N_EXPERTS), axis=0, keepdims=True)
        vals.append(m)
        ids.append(ik)
        work = jnp.where(e_io == ik, -jnp.inf, work)
    exps = [jnp.exp(v - vals[0]) for v in vals]
    denom = exps[0] + exps[1] + exps[2] + exps[3]
    wts = [e / denom for e in exps]

    sel = [e_io == ik for ik in ids]
    oh = jnp.where(sel[0] | sel[1] | sel[2] | sel[3], 1.0, 0.0)
    r_io = lax.broadcasted_iota(jnp.int32, (TM, TM), 0)
    c_io = lax.broadcasted_iota(jnp.int32, (TM, TM), 1)
    upper = jnp.where(r_io < c_io, 1.0, 0.0).astype(BF16)
    cum = jnp.dot(oh.astype(BF16), upper, preferred_element_type=F32) + base_ref[...]
    ranks = [jnp.sum(jnp.where(sk, cum, 0.0), axis=0, keepdims=True) for sk in sel]
    base_ref[...] = base_ref[...] + jnp.sum(oh, axis=1, keepdims=True)
    cnt_ref[...] = base_ref[:, :128]

    row8 = lax.broadcasted_iota(jnp.int32, (8, TM), 0)

    def stack4(parts, fill):
        out = jnp.full((8, TM), fill, parts[0].dtype)
        for kk in range(TOP_K):
            out = jnp.where(row8 == kk, parts[kk], out)
        return out

    idx_ref[...] = stack4(ids, 0)
    wgt_ref[...] = stack4(wts, 0.0)
    rank_ref[...] = stack4([r.astype(jnp.int32) for r in ranks], 0)


def _mixer_call(layer, x, mods, g_mix, g_ffn, w_in_b, ln_g, ln_b, w_sp, bs_full, rn_g,
                w_out_b, cos_t, sin_t, dmat, qdec, kdec, cdec, wr_t, br):
    bsz, seq, d = x.shape
    n_tok = bsz * seq
    nj = seq // TM
    shm, scm, gtm, shf, scf = mods
    vec = lambda: pl.BlockSpec((1, 1, d), lambda b, j: (b, 0, 0))
    lvec = lambda w: pl.BlockSpec((1, 1, w), lambda b, j: (layer, 0, 0))
    full = lambda shp: pl.BlockSpec(shp, lambda b, j: (0,) * len(shp))
    tok = lambda shp: pl.BlockSpec(shp, lambda b, j: (0, b * nj + j))
    in_specs = [
        pl.BlockSpec((1, TM, d), lambda b, j: (b, j, 0)),
        vec(), vec(), vec(), vec(), vec(),
        lvec(d), lvec(d),
        pl.BlockSpec((1, d, 6 * GROUP_W), lambda b, j: (layer, 0, 0)),
        lvec(GROUP_W), lvec(GROUP_W),
        pl.BlockSpec((1, N_HEADS, CHUNK, CHUNK), lambda b, j: (layer, 0, 0, 0)),
        pl.BlockSpec((1, CHUNK, GROUP_W), lambda b, j: (layer, 0, 0)),
        lvec(GROUP_W),
        pl.BlockSpec((1, d, d), lambda b, j: (layer, 0, 0)),
        pl.BlockSpec((TM, HEAD_DIM), lambda b, j: (j, 0)),
        pl.BlockSpec((TM, HEAD_DIM), lambda b, j: (j, 0)),
        full((N_HEADS, CHUNK, CHUNK)),
        full((CHUNK, GROUP_W)), full((CHUNK, GROUP_W)), full((1, GROUP_W)),
        pl.BlockSpec((1, N_EXPERTS, d), lambda b, j: (layer, 0, 0)),
        pl.BlockSpec((1, N_EXPERTS, 1), lambda b, j: (layer, 0, 0)),
    ]
    out_specs = [
        pl.BlockSpec((1, TM, d), lambda b, j: (b, j, 0)),
        pl.BlockSpec((TM, d), lambda b, j: (b * nj + j, 0)),
        tok((8, TM)), tok((8, TM)), tok((8, TM)),
        pl.BlockSpec((N_EXPERTS, 128), lambda b, j: (0, 0)),
    ]
    out_shape = [
        jax.ShapeDtypeStruct((bsz, seq, d), F32),
        jax.ShapeDtypeStruct((n_tok, d), F32),
        jax.ShapeDtypeStruct((8, n_tok), jnp.int32),
        jax.ShapeDtypeStruct((8, n_tok), F32),
        jax.ShapeDtypeStruct((8, n_tok), jnp.int32),
        jax.ShapeDtypeStruct((N_EXPERTS, 128), F32),
    ]
    return pl.pallas_call(
        _mixer_kernel,
        grid=(bsz, nj),
        in_specs=in_specs,
        out_specs=out_specs,
        out_shape=out_shape,
        scratch_shapes=[
            pltpu.VMEM((N_HEADS, HEAD_DIM, HEAD_DIM), F32),
            pltpu.VMEM((N_EXPERTS, TM), F32),
            pltpu.VMEM((TM, d), BF16),
        ],
        compiler_params=pltpu.CompilerParams(
            dimension_semantics=("arbitrary", "arbitrary"), vmem_limit_bytes=VMEM_LIMIT),
        name="mixer_router",
    )(x, shm, scm, gtm, shf, scf, g_mix, g_ffn, w_in_b, ln_g, ln_b, w_sp, bs_full, rn_g,
      w_out_b, cos_t, sin_t, dmat, qdec, kdec, cdec, wr_t, br)


def _dispatch_kernel(cnt_ref, pstart_ref, padded_ref, slot_ref, h2_hbm, xs_hbm, zblk, sem, zsem):
    i = pl.program_id(0)
    base = i * TD

    def issue(t, carry):
        for kk in range(TOP_K):
            s = slot_ref[kk * TD + t]
            pltpu.make_async_copy(h2_hbm.at[pl.ds(base + t, 1)], xs_hbm.at[pl.ds(s, 1)], sem).start()
        return carry

    lax.fori_loop(0, TD, issue, 0)

    @pl.when(i == 0)
    def _():
        zblk[...] = jnp.zeros_like(zblk)

        def per_expert(e, carry):
            start = pstart_ref[e]

            def fill(r, c2):
                pltpu.make_async_copy(zblk.at[pl.ds(0, 1)], xs_hbm.at[pl.ds(start + r, 1)], zsem).start()
                return c2

            lax.fori_loop(cnt_ref[e], padded_ref[e], fill, 0)

            def drain(r, c2):
                pltpu.make_async_copy(zblk.at[pl.ds(0, 1)], xs_hbm.at[pl.ds(start + r, 1)], zsem).wait()
                return c2

            lax.fori_loop(cnt_ref[e], padded_ref[e], drain, 0)
            return carry

        lax.fori_loop(0, N_EXPERTS, per_expert, 0)

        first_tail = (pstart_ref[N_EXPERTS - 1] + padded_ref[N_EXPERTS - 1]) // BM

        def tail(blk, carry):
            cp = pltpu.make_async_copy(zblk, xs_hbm.at[pl.ds(blk * BM, BM)], zsem)
            cp.start()
            cp.wait()
            return carry

        lax.fori_loop(first_tail, xs_hbm.shape[0] // BM, tail, 0)

    pltpu.make_async_copy(h2_hbm.at[pl.ds(0, TOP_K * TD)], xs_hbm.at[pl.ds(0, TOP_K * TD)], sem).wait()


def _dispatch_call(counts, pstart, padded, slot_tiles, h2, n_slots):
    n_tok, d = h2.shape
    grid_spec = pltpu.PrefetchScalarGridSpec(
        num_scalar_prefetch=3,
        grid=(n_tok // TD,),
        in_specs=[
            pl.BlockSpec((TOP_K * TD,), lambda i, *_: (i,), memory_space=pltpu.SMEM),
            pl.BlockSpec(memory_space=pl.ANY),
        ],
        out_specs=pl.BlockSpec(memory_space=pl.ANY),
        scratch_shapes=[pltpu.VMEM((BM, d), F32), pltpu.SemaphoreType.DMA, pltpu.SemaphoreType.DMA],
    )
    return pl.pallas_call(
        _dispatch_kernel,
        grid_spec=grid_spec,
        out_shape=jax.ShapeDtypeStruct((n_slots, d), F32),
        compiler_params=pltpu.CompilerParams(dimension_semantics=("arbitrary",)),
        name="moe_dispatch",
    )(counts, pstart, padded, slot_tiles, h2)


def _ffn_kernel(be_ref, bsrc_ref, bvalid_ref, xs_ref, w1_ref, b1_ref, w2_ref, b2_ref, y_ref):
    i = pl.program_id(0)

    @pl.when(bvalid_ref[i] == 1)
    def _():
        a = jnp.dot(xs_ref[...].astype(BF16), w1_ref[0], preferred_element_type=F32) + b1_ref[0]
        a_glu = jnp.minimum(a[:, :EXPERT_FF], SWIGLU_LIMIT)
        a_lin = jnp.clip(a[:, EXPERT_FF:], -SWIGLU_LIMIT, SWIGLU_LIMIT)
        act = a_glu * jax.nn.sigmoid(SWIGLU_ALPHA * a_glu) * (a_lin + 1.0)
        y_ref[...] = jnp.dot(act.astype(BF16), w2_ref[0], preferred_element_type=F32) + b2_ref[0]

    @pl.when(bvalid_ref[i] == 0)
    def _():
        y_ref[...] = jnp.zeros_like(y_ref)


def _ffn_call(block_expert, block_src, block_valid, xs, w1_b, b1, w2_b, b2):
    n_slots, d = xs.shape
    n_blocks = n_slots // BM
    grid_spec = pltpu.PrefetchScalarGridSpec(
        num_scalar_prefetch=3,
        grid=(n_blocks,),
        in_specs=[
            pl.BlockSpec((BM, d), lambda i, be, bs, bv: (bs[i], 0)),
            pl.BlockSpec((1, d, 2 * EXPERT_FF), lambda i, be, bs, bv: (be[i], 0, 0)),
            pl.BlockSpec((1, 1, 2 * EXPERT_FF), lambda i, be, bs, bv: (be[i], 0, 0)),
            pl.BlockSpec((1, EXPERT_FF, d), lambda i, be, bs, bv: (be[i], 0, 0)),
            pl.BlockSpec((1, 1, d), lambda i, be, bs, bv: (be[i], 0, 0)),
        ],
        out_specs=pl.BlockSpec((BM, d), lambda i, be, bs, bv: (i, 0)),
    )
    return pl.pallas_call(
        _ffn_kernel,
        grid_spec=grid_spec,
        out_shape=jax.ShapeDtypeStruct((n_slots, d), F32),
        compiler_params=pltpu.CompilerParams(
            dimension_semantics=("arbitrary",), vmem_limit_bytes=VMEM_LIMIT),
        name="moe_ffn",
    )(block_expert, block_src, block_valid, xs, w1_b, b1, w2_b, b2)


def _combine_kernel(slot_ref, w_ref, x1_ref, gt_ref, gfin_ref, y_hbm, o_ref, ybuf, sem, *, final):
    def issue(t, carry):
        for kk in range(TOP_K):
            s = slot_ref[kk * TD + t]
            pltpu.make_async_copy(y_hbm.at[pl.ds(s, 1)], ybuf.at[kk, pl.ds(t, 1)], sem).start()
        return carry

    lax.fori_loop(0, TD, issue, 0)
    for kk in range(TOP_K):
        pltpu.make_async_copy(y_hbm.at[pl.ds(0, TD)], ybuf.at[kk], sem).wait()

    w = w_ref[...]
    acc = w[:, 0:1] * ybuf[0]
    for kk in range(1, TOP_K):
        acc = acc + w[:, kk:kk + 1] * ybuf[kk]
    out = x1_ref[...] + gt_ref[0] * acc
    if final:
        out = out * lax.rsqrt(jnp.mean(out * out, axis=-1, keepdims=True) + EPS) * gfin_ref[...]
    o_ref[...] = out


def _combine_call(slot_tiles, w_tok, x1, gtf, g_final, y, seq, final):
    n_tok, d = x1.shape
    per_b = seq // TD
    return pl.pallas_call(
        functools.partial(_combine_kernel, final=final),
        grid=(n_tok // TD,),
        in_specs=[
            pl.BlockSpec((TOP_K * TD,), lambda i: (i,), memory_space=pltpu.SMEM),
            pl.BlockSpec((TD, TOP_K), lambda i: (i, 0)),
            pl.BlockSpec((TD, d), lambda i: (i, 0)),
            pl.BlockSpec((1, 1, d), lambda i: (i // per_b, 0, 0)),
            pl.BlockSpec((1, d), lambda i: (0, 0)),
            pl.BlockSpec(memory_space=pl.ANY),
        ],
        out_specs=pl.BlockSpec((TD, d), lambda i: (i, 0)),
        out_shape=jax.ShapeDtypeStruct((n_tok, d), F32),
        scratch_shapes=[pltpu.VMEM((TOP_K, TD, d), F32), pltpu.SemaphoreType.DMA],
        compiler_params=pltpu.CompilerParams(
            dimension_semantics=("arbitrary",), vmem_limit_bytes=VMEM_LIMIT),
        name="moe_combine",
    )(slot_tiles, w_tok, x1, gtf, g_final, y)


def _tables(seq):
    half = HEAD_DIM // 2
    inv_freq = ROPE_BASE ** (-jnp.arange(half, dtype=F32) / half)
    ang = jnp.arange(seq, dtype=F32)[:, None] * inv_freq[None, :]
    cos = jnp.cos(ang)
    sin = jnp.sin(ang)
    cos_t = jnp.concatenate([cos, cos], axis=-1)
    sin_t = jnp.concatenate([-sin, sin], axis=-1)
    gamma = 1.0 - jnp.exp2(-5.0 - jnp.arange(N_HEADS, dtype=F32))
    log_g = jnp.log(gamma)
    idx = jnp.arange(CHUNK, dtype=F32)
    diff = idx[:, None] - idx[None, :]
    dmat = jnp.where(diff[None] >= 0,
                     jnp.exp(log_g[:, None, None] * jnp.maximum(diff, 0.0)[None]), 0.0)
    k_decay = jnp.exp(log_g[:, None] * (CHUNK - 1.0 - idx)[None, :])
    q_decay = jnp.exp(log_g[:, None] * (idx + 1.0)[None, :])
    chunk_decay = jnp.exp(log_g * CHUNK)
    widen = lambda t: jnp.repeat(t.T[:, :, None], HEAD_DIM, axis=2).reshape(CHUNK, GROUP_W)
    qdec = widen(q_decay)
    kdec = widen(k_decay)
    cdec = jnp.repeat(chunk_decay[:, None], HEAD_DIM, axis=1).reshape(1, GROUP_W)
    return cos_t, sin_t, dmat, qdec, kdec, cdec


def kernel(x, c, w_ada, b_ada, g_mix, w_in, gmlp_ln_g, gmlp_ln_b, w_spatial, b_spatial,
           ret_norm_g, w_out, g_ffn, w_router, b_router, w1, b1, w2, b2, g_final):
    bsz, seq, d = x.shape
    depth = w_in.shape[0]
    n_tok = bsz * seq
    n_assign = n_tok * TOP_K
    n_slots = n_assign + N_EXPERTS * BM
    n_blocks = n_slots // BM

    mod = _ada_call(c, w_ada, b_ada)
    cos_t, sin_t, dmat, qdec, kdec, cdec = _tables(seq)

    w_in_b = w_in.astype(BF16)
    w_out_b = w_out.astype(BF16)
    wr_t = jnp.swapaxes(w_router, 1, 2).astype(BF16)
    br = b_router.reshape(depth, N_EXPERTS, 1)
    w1_b = w1.astype(BF16).reshape(depth * N_EXPERTS, d, 2 * EXPERT_FF)
    w2_b = w2.astype(BF16).reshape(depth * N_EXPERTS, EXPERT_FF, d)
    b1_r = b1.reshape(depth * N_EXPERTS, 1, 2 * EXPERT_FF)
    b2_r = b2.reshape(depth * N_EXPERTS, 1, d)
    bs_full = jnp.repeat(jnp.swapaxes(b_spatial, 1, 2)[:, :, :, None], HEAD_DIM, axis=3)
    bs_full = bs_full.reshape(depth, CHUNK, GROUP_W)
    row3 = lambda a: a.reshape(depth, 1, a.shape[-1])
    gfin = g_final.reshape(1, d)
    e_ids = jnp.arange(N_EXPERTS, dtype=jnp.int32)

    for l in range(depth):
        mods = [m.reshape(bsz, 1, d) for m in jnp.split(mod[l], N_MOD, axis=-1)]
        shm, scm, gtm, shf, scf, gtf = mods
        x1, h2, idx8, wgt8, rank8, cnt = _mixer_call(
            l, x, (shm, scm, gtm, shf, scf), row3(g_mix), row3(g_ffn), w_in_b,
            row3(gmlp_ln_g), row3(gmlp_ln_b), w_spatial, bs_full, row3(ret_norm_g), w_out_b,
            cos_t, sin_t, dmat, qdec, kdec, cdec, wr_t, br)

        counts = cnt[:, 0].astype(jnp.int32)
        padded = ((counts + BM - 1) // BM) * BM
        pend = jnp.cumsum(padded)
        pstart = pend - padded
        idx = idx8[:TOP_K]
        off = jnp.sum(jnp.where(idx[:, :, None] == e_ids, pstart, 0), axis=-1)
        slot = off + rank8[:TOP_K]
        slot_tiles = slot.reshape(TOP_K, n_tok // TD, TD).transpose(1, 0, 2).reshape(-1)
        blk = jnp.arange(n_blocks, dtype=jnp.int32)
        n_used = pend[-1] // BM
        block_valid = (blk < n_used).astype(jnp.int32)
        block_expert = jnp.minimum(
            jnp.searchsorted(pend, blk * BM, side='right'), N_EXPERTS - 1).astype(jnp.int32)
        block_expert = block_expert + l * N_EXPERTS
        block_src = jnp.minimum(blk, n_used - 1).astype(jnp.int32)

        xs = _dispatch_call(counts, pstart.astype(jnp.int32), (padded).astype(jnp.int32),
                            slot_tiles, h2, n_slots)
        y = _ffn_call(block_expert, block_src, block_valid, xs, w1_b, b1_r, w2_b, b2_r)
        w_tok = wgt8[:TOP_K].T
        x = _combine_call(slot_tiles, w_tok, x1.reshape(n_tok, d), gtf, gfin, y, seq,
                          final=(l == depth - 1)).reshape(bsz, seq, d)
    return x
```

```python
import functools

import jax
import jax.numpy as jnp
from jax import lax
from jax.experimental import pallas as pl
from jax.experimental.pallas import tpu as pltpu

F32 = jnp.float32
BF16 = jnp.bfloat16

D_MODEL = 1024
CHUNK = 128
N_HEADS = 4
HEAD_DIM = 128
GROUP_W = N_HEADS * HEAD_DIM
N_EXPERTS = 32
TOP_K = 4
EXPERT_FF = 1024
N_MOD = 6
EPS = 1e-5
ROPE_BASE = 10000.0
SWIGLU_ALPHA = 1.702
SWIGLU_LIMIT = 7.0

TM = 256
BM = 256
TD = 256
ISSUE_UNROLL = 8
VMEM_LIMIT = 56 * 1024 * 1024


def _gelu_tanh(x):
    return x * (0.5 * (1.0 + jnp.tanh(0.7978845608028654 * (x + 0.044715 * (x * x * x)))))


def _silu(x):
    return x * jax.nn.sigmoid(x)


def _ada_kernel(c_ref, w_ref, b_ref, o_ref):
    ca = _silu(c_ref[...]).astype(BF16)
    o_ref[0] = jnp.dot(ca, w_ref[0].astype(BF16), preferred_element_type=F32) + b_ref[0]


def _ada_call(c, w_ada, b_ada):
    depth, d, _ = w_ada.shape
    bsz = c.shape[0]
    return pl.pallas_call(
        _ada_kernel,
        grid=(depth, N_MOD),
        in_specs=[
            pl.BlockSpec((bsz, d), lambda l, n: (0, 0)),
            pl.BlockSpec((1, d, d), lambda l, n: (l, 0, n)),
            pl.BlockSpec((1, 1, d), lambda l, n: (l, 0, n)),
        ],
        out_specs=pl.BlockSpec((1, bsz, d), lambda l, n: (l, 0, n)),
        out_shape=jax.ShapeDtypeStruct((depth, bsz, N_MOD * d), F32),
        name="adaln_mod",
    )(c, w_ada, b_ada.reshape(depth, 1, N_MOD * d))


def _mixer_kernel(x_ref, shm_ref, scm_ref, gtm_ref, shf_ref, scf_ref, gmix_ref, gffn_ref,
                  win_ref, lng_ref, lnb_ref, ws_ref, bs_ref, rng_ref, wout_ref,
                  cos_ref, sin_ref, dmat_ref, qdec_ref, kdec_ref, cdec_ref, wr_ref, br_ref,
                  x1_ref, h2_ref, idx_ref, wgt_ref, rank_ref, cnt_ref,
                  state_ref, base_ref, ob_ref):
    b = pl.program_id(0)
    j = pl.program_id(1)
    n_chunks = TM // CHUNK

    @pl.when(j == 0)
    def _():
        state_ref[...] = jnp.zeros_like(state_ref)

    @pl.when((b == 0) & (j == 0))
    def _():
        base_ref[...] = jnp.zeros_like(base_ref)

    x = x_ref[0]
    ms = jnp.mean(x * x, axis=-1, keepdims=True)
    h = x * lax.rsqrt(ms + EPS) * (gmix_ref[0] * (1.0 + scm_ref[0])) + shm_ref[0]
    hb = h.astype(BF16)

    def proj(k):
        return jnp.dot(hb, win_ref[0, :, k * GROUP_W:(k + 1) * GROUP_W],
                       preferred_element_type=F32)

    def rows(c):
        return slice(c * CHUNK, (c + 1) * CHUNK)

    def lanes(hh):
        return slice(hh * HEAD_DIM, (hh + 1) * HEAD_DIM)

    gu = _gelu_tanh(proj(0))
    gv = _gelu_tanh(proj(1))
    mu = jnp.mean(gv, axis=-1, keepdims=True)
    vc = gv - mu
    var = jnp.mean(vc * vc, axis=-1, keepdims=True)
    vn = (vc * lax.rsqrt(var + EPS) * lng_ref[0] + lnb_ref[0]).astype(BF16)
    t_io = lax.broadcasted_iota(jnp.int32, (CHUNK, CHUNK), 0)
    s_io = lax.broadcasted_iota(jnp.int32, (CHUNK, CHUNK), 1)
    causal = t_io >= s_io
    bias = bs_ref[0]
    for hh in range(N_HEADS):
        wm = jnp.where(causal, ws_ref[0, hh], 0.0).astype(BF16)
        for c in range(n_chunks):
            mixed = jnp.dot(wm, vn[rows(c), lanes(hh)], preferred_element_type=F32)
            oa = gu[rows(c), lanes(hh)] * (mixed + bias[:, lanes(hh)])
            ob_ref[rows(c), lanes(hh)] = oa.astype(BF16)

    q = proj(2)
    k = proj(3)
    vr = proj(4).astype(BF16)
    g = proj(5)
    cos = cos_ref[...]
    sin = sin_ref[...]
    k_scale = HEAD_DIM ** -0.5
    for hh in range(N_HEADS):
        qh = q[:, lanes(hh)]
        kh = k[:, lanes(hh)]
        qh = qh * cos + pltpu.roll(qh, HEAD_DIM // 2, 1) * sin
        kh = (kh * cos + pltpu.roll(kh, HEAD_DIM // 2, 1) * sin) * k_scale
        dm = dmat_ref[hh]
        qd = qdec_ref[:, lanes(hh)]
        kd = kdec_ref[:, lanes(hh)]
        cd = cdec_ref[:, lanes(hh)]
        gn = rng_ref[0][:, lanes(hh)]
        for c in range(n_chunks):
            qc = qh[rows(c)]
            kc = kh[rows(c)]
            vcb = vr[rows(c), lanes(hh)]
            s = lax.dot_general(qc.astype(BF16), kc.astype(BF16),
                                (((1,), (1,)), ((), ())), preferred_element_type=F32) * dm
            st = state_ref[hh]
            o = jnp.dot(s.astype(BF16), vcb, preferred_element_type=F32)
            o = o + jnp.dot((qc * qd).astype(BF16), st.astype(BF16), preferred_element_type=F32)
            kdt = (kc * kd).T.astype(BF16)
            state_ref[hh] = st * cd + jnp.dot(kdt, vcb, preferred_element_type=F32)
            o = o * lax.rsqrt(jnp.mean(o * o, axis=-1, keepdims=True) + EPS) * gn
            ob = o * _silu(g[rows(c), lanes(hh)])
            ob_ref[rows(c), GROUP_W + hh * HEAD_DIM:GROUP_W + (hh + 1) * HEAD_DIM] = ob.astype(BF16)

    y = jnp.dot(ob_ref[...], wout_ref[0], preferred_element_type=F32)
    x1 = x + gtm_ref[0] * y
    x1_ref[0] = x1

    ms2 = jnp.mean(x1 * x1, axis=-1, keepdims=True)
    h2 = x1 * lax.rsqrt(ms2 + EPS) * (gffn_ref[0] * (1.0 + scf_ref[0])) + shf_ref[0]
    h2_ref[...] = h2

    lg = lax.dot_general(wr_ref[0], h2.astype(BF16), (((1,), (1,)), ((), ())),
                         preferred_element_type=F32) + br_ref[0]
    e_io = lax.broadcasted_iota(jnp.int32, (N_EXPERTS, TM), 0)
    work = lg
    vals, ids = [], []
    for _ in range(TOP_K):
        m = jnp.max(work, axis=0, keepdims=True)
        ik = jnp.min(jnp.where(work == m, e_io, N_EXPERTS), axis=0, keepdims=True)
        vals.append(m)
        ids.append(ik)
        work = jnp.where(e_io == ik, -jnp.inf, work)
    exps = [jnp.exp(v - vals[0]) for v in vals]
    denom = exps[0] + exps[1] + exps[2] + exps[3]
    wts = [e / denom for e in exps]

    sel = [e_io == ik for ik in ids]
    oh = jnp.where(sel[0] | sel[1] | sel[2] | sel[3], 1.0, 0.0)
    r_io = lax.broadcasted_iota(jnp.int32, (TM, TM), 0)
    c_io = lax.broadcasted_iota(jnp.int32, (TM, TM), 1)
    upper = jnp.where(r_io < c_io, 1.0, 0.0).astype(BF16)
    cum = jnp.dot(oh.astype(BF16), upper, preferred_element_type=F32) + base_ref[...]
    ranks = [jnp.sum(jnp.where(sk, cum, 0.0), axis=0, keepdims=True) for sk in sel]
    base_ref[...] = base_ref[...] + jnp.sum(oh, axis=1, keepdims=True)
    cnt_ref[...] = base_ref[:, :128]

    row8 = lax.broadcasted_iota(jnp.int32, (8, TM), 0)

    def stack4(parts, fill):
        out = jnp.full((8, TM), fill, parts[0].dtype)
        for kk in range(TOP_K):
            out = jnp.where(row8 == kk, parts[kk], out)
        return out

    idx_ref[...] = stack4(ids, 0)
    wgt_ref[...] = stack4(wts, 0.0)
    rank_ref[...] = stack4([r.astype(jnp.int32) for r in ranks], 0)


def _mixer_call(layer, x, mods, g_mix, g_ffn, w_in_b, ln_g, ln_b, w_sp, bs_full, rn_g,
                w_out_b, cos_t, sin_t, dmat, qdec, kdec, cdec, wr_t, br):
    bsz, seq, d = x.shape
    n_tok = bsz * seq
    nj = seq // TM
    shm, scm, gtm, shf, scf = mods
    vec = lambda: pl.BlockSpec((1, 1, d), lambda b, j: (b, 0, 0))
    lvec = lambda w: pl.BlockSpec((1, 1, w), lambda b, j: (layer, 0, 0))
    full = lambda shp: pl.BlockSpec(shp, lambda b, j: (0,) * len(shp))
    tok = lambda shp: pl.BlockSpec(shp, lambda b, j: (0, b * nj + j))
    in_specs = [
        pl.BlockSpec((1, TM, d), lambda b, j: (b, j, 0)),
        vec(), vec(), vec(), vec(), vec(),
        lvec(d), lvec(d),
        pl.BlockSpec((1, d, 6 * GROUP_W), lambda b, j: (layer, 0, 0)),
        lvec(GROUP_W), lvec(GROUP_W),
        pl.BlockSpec((1, N_HEADS, CHUNK, CHUNK), lambda b, j: (layer, 0, 0, 0)),
        pl.BlockSpec((1, CHUNK, GROUP_W), lambda b, j: (layer, 0, 0)),
        lvec(GROUP_W),
        pl.BlockSpec((1, d, d), lambda b, j: (layer, 0, 0)),
        pl.BlockSpec((TM, HEAD_DIM), lambda b, j: (j, 0)),
        pl.BlockSpec((TM, HEAD_DIM), lambda b, j: (j, 0)),
        full((N_HEADS, CHUNK, CHUNK)),
        full((CHUNK, GROUP_W)), full((CHUNK, GROUP_W)), full((1, GROUP_W)),
        pl.BlockSpec((1, N_EXPERTS, d), lambda b, j: (layer, 0, 0)),
        pl.BlockSpec((1, N_EXPERTS, 1), lambda b, j: (layer, 0, 0)),
    ]
    out_specs = [
        pl.BlockSpec((1, TM, d), lambda b, j: (b, j, 0)),
        pl.BlockSpec((TM, d), lambda b, j: (b * nj + j, 0)),
        tok((8, TM)), tok((8, TM)), tok((8, TM)),
        pl.BlockSpec((N_EXPERTS, 128), lambda b, j: (0, 0)),
    ]
    out_shape = [
        jax.ShapeDtypeStruct((bsz, seq, d), F32),
        jax.ShapeDtypeStruct((n_tok, d), F32),
        jax.ShapeDtypeStruct((8, n_tok), jnp.int32),
        jax.ShapeDtypeStruct((8, n_tok), F32),
        jax.ShapeDtypeStruct((8, n_tok), jnp.int32),
        jax.ShapeDtypeStruct((N_EXPERTS, 128), F32),
    ]
    return pl.pallas_call(
        _mixer_kernel,
        grid=(bsz, nj),
        in_specs=in_specs,
        out_specs=out_specs,
        out_shape=out_shape,
        scratch_shapes=[
            pltpu.VMEM((N_HEADS, HEAD_DIM, HEAD_DIM), F32),
            pltpu.VMEM((N_EXPERTS, TM), F32),
            pltpu.VMEM((TM, d), BF16),
        ],
        compiler_params=pltpu.CompilerParams(
            dimension_semantics=("arbitrary", "arbitrary"), vmem_limit_bytes=VMEM_LIMIT),
        name="mixer_router",
    )(x, shm, scm, gtm, shf, scf, g_mix, g_ffn, w_in_b, ln_g, ln_b, w_sp, bs_full, rn_g,
      w_out_b, cos_t, sin_t, dmat, qdec, kdec, cdec, wr_t, br)


def _dispatch_kernel(cnt_ref, pstart_ref, padded_ref, slot_ref, h2_ref, xs_hbm, zblk, sem, zsem):
    i = pl.program_id(0)

    def issue(g, carry):
        t0 = pl.multiple_of(g * ISSUE_UNROLL, ISSUE_UNROLL)
        for u in range(ISSUE_UNROLL):
            for kk in range(TOP_K):
                s = slot_ref[kk * TD + t0 + u]
                pltpu.make_async_copy(h2_ref.at[pl.ds(t0 + u, 1)], xs_hbm.at[pl.ds(s, 1)], sem).start()
        return carry

    lax.fori_loop(0, TD // ISSUE_UNROLL, issue, 0)

    @pl.when(i == 0)
    def _():
        zblk[...] = jnp.zeros_like(zblk)

        def per_expert(e, carry):
            start = pstart_ref[e]

            def fill(r, c2):
                pltpu.make_async_copy(zblk.at[pl.ds(0, 1)], xs_hbm.at[pl.ds(start + r, 1)], zsem).start()
                return c2

            lax.fori_loop(cnt_ref[e], padded_ref[e], fill, 0)

            def drain(r, c2):
                pltpu.make_async_copy(zblk.at[pl.ds(0, 1)], xs_hbm.at[pl.ds(start + r, 1)], zsem).wait()
                return c2

            lax.fori_loop(cnt_ref[e], padded_ref[e], drain, 0)
            return carry

        lax.fori_loop(0, N_EXPERTS, per_expert, 0)

        first_tail = (pstart_ref[N_EXPERTS - 1] + padded_ref[N_EXPERTS - 1]) // BM

        def tail(blk, carry):
            cp = pltpu.make_async_copy(zblk, xs_hbm.at[pl.ds(blk * BM, BM)], zsem)
            cp.start()
            cp.wait()
            return carry

        lax.fori_loop(first_tail, xs_hbm.shape[0] // BM, tail, 0)

    for _ in range(TOP_K):
        pltpu.make_async_copy(h2_ref, xs_hbm.at[pl.ds(0, TD)], sem).wait()


def _dispatch_call(counts, pstart, padded, slot_tiles, h2, n_slots):
    n_tok, d = h2.shape
    grid_spec = pltpu.PrefetchScalarGridSpec(
        num_scalar_prefetch=3,
        grid=(n_tok // TD,),
        in_specs=[
            pl.BlockSpec((TOP_K * TD,), lambda i, *_: (i,), memory_space=pltpu.SMEM),
            pl.BlockSpec((TD, d), lambda i, *_: (i, 0)),
        ],
        out_specs=pl.BlockSpec(memory_space=pl.ANY),
        scratch_shapes=[pltpu.VMEM((BM, d), F32), pltpu.SemaphoreType.DMA, pltpu.SemaphoreType.DMA],
    )
    return pl.pallas_call(
        _dispatch_kernel,
        grid_spec=grid_spec,
        out_shape=jax.ShapeDtypeStruct((n_slots, d), F32),
        compiler_params=pltpu.CompilerParams(dimension_semantics=("arbitrary",)),
        name="moe_dispatch",
    )(counts, pstart, padded, slot_tiles, h2)


def _ffn_kernel(be_ref, bsrc_ref, bvalid_ref, xs_ref, w1_ref, b1_ref, w2_ref, b2_ref, y_ref):
    i = pl.program_id(0)

    @pl.when(bvalid_ref[i] == 1)
    def _():
        a = jnp.dot(xs_ref[...].astype(BF16), w1_ref[0], preferred_element_type=F32) + b1_ref[0]
        a_glu = jnp.minimum(a[:, :EXPERT_FF], SWIGLU_LIMIT)
        a_lin = jnp.clip(a[:, EXPERT_FF:], -SWIGLU_LIMIT, SWIGLU_LIMIT)
        act = a_glu * jax.nn.sigmoid(SWIGLU_ALPHA * a_glu) * (a_lin + 1.0)
        y_ref[...] = jnp.dot(act.astype(BF16), w2_ref[0], preferred_element_type=F32) + b2_ref[0]

    @pl.when(bvalid_ref[i] == 0)
    def _():
        y_ref[...] = jnp.zeros_like(y_ref)


def _ffn_call(block_expert, block_src, block_valid, xs, w1_b, b1, w2_b, b2):
    n_slots, d = xs.shape
    n_blocks = n_slots // BM
    grid_spec = pltpu.PrefetchScalarGridSpec(
        num_scalar_prefetch=3,
        grid=(n_blocks,),
        in_specs=[
            pl.BlockSpec((BM, d), lambda i, be, bs, bv: (bs[i], 0)),
            pl.BlockSpec((1, d, 2 * EXPERT_FF), lambda i, be, bs, bv: (be[i], 0, 0)),
            pl.BlockSpec((1, 1, 2 * EXPERT_FF), lambda i, be, bs, bv: (be[i], 0, 0)),
            pl.BlockSpec((1, EXPERT_FF, d), lambda i, be, bs, bv: (be[i], 0, 0)),
            pl.BlockSpec((1, 1, d), lambda i, be, bs, bv: (be[i], 0, 0)),
        ],
        out_specs=pl.BlockSpec((BM, d), lambda i, be, bs, bv: (i, 0)),
    )
    return pl.pallas_call(
        _ffn_kernel,
        grid_spec=grid_spec,
        out_shape=jax.ShapeDtypeStruct((n_slots, d), F32),
        compiler_params=pltpu.CompilerParams(
            dimension_semantics=("arbitrary",), vmem_limit_bytes=VMEM_LIMIT),
        name="moe_ffn",
    )(block_expert, block_src, block_valid, xs, w1_b, b1, w2_b, b2)


def _combine_kernel(slot_ref, w_ref, x1_ref, gt_ref, gfin_ref, y_hbm, o_ref, ybuf, sem, *, final):
    def issue(g, carry):
        t0 = pl.multiple_of(g * ISSUE_UNROLL, ISSUE_UNROLL)
        for u in range(ISSUE_UNROLL):
            for kk in range(TOP_K):
                s = slot_ref[kk * TD + t0 + u]
                pltpu.make_async_copy(y_hbm.at[pl.ds(s, 1)], ybuf.at[kk, pl.ds(t0 + u, 1)], sem).start()
        return carry

    lax.fori_loop(0, TD // ISSUE_UNROLL, issue, 0)
    for kk in range(TOP_K):
        pltpu.make_async_copy(y_hbm.at[pl.ds(0, TD)], ybuf.at[kk], sem).wait()

    w = w_ref[...]
    acc = w[:, 0:1] * ybuf[0]
    for kk in range(1, TOP_K):
        acc = acc + w[:, kk:kk + 1] * ybuf[kk]
    out = x1_ref[...] + gt_ref[0] * acc
    if final:
        out = out * lax.rsqrt(jnp.mean(out * out, axis=-1, keepdims=True) + EPS) * gfin_ref[...]
    o_ref[...] = out


def _combine_call(slot_tiles, w_tok, x1, gtf, g_final, y, seq, final):
    n_tok, d = x1.shape
    per_b = seq // TD
    return pl.pallas_call(
        functools.partial(_combine_kernel, final=final),
        grid=(n_tok // TD,),
        in_specs=[
            pl.BlockSpec((TOP_K * TD,), lambda i: (i,), memory_space=pltpu.SMEM),
            pl.BlockSpec((TD, TOP_K), lambda i: (i, 0)),
            pl.BlockSpec((TD, d), lambda i: (i, 0)),
            pl.BlockSpec((1, 1, d), lambda i: (i // per_b, 0, 0)),
            pl.BlockSpec((1, d), lambda i: (0, 0)),
            pl.BlockSpec(memory_space=pl.ANY),
        ],
        out_specs=pl.BlockSpec((TD, d), lambda i: (i, 0)),
        out_shape=jax.ShapeDtypeStruct((n_tok, d), F32),
        scratch_shapes=[pltpu.VMEM((TOP_K, TD, d), F32), pltpu.SemaphoreType.DMA],
        compiler_params=pltpu.CompilerParams(
            dimension_semantics=("arbitrary",), vmem_limit_bytes=VMEM_LIMIT),
        name="moe_combine",
    )(slot_tiles, w_tok, x1, gtf, g_final, y)


def _tables(seq):
    half = HEAD_DIM // 2
    inv_freq = ROPE_BASE ** (-jnp.arange(half, dtype=F32) / half)
    ang = jnp.arange(seq, dtype=F32)[:, None] * inv_freq[None, :]
    cos = jnp.cos(ang)
    sin = jnp.sin(ang)
    cos_t = jnp.concatenate([cos, cos], axis=-1)
    sin_t = jnp.concatenate([-sin, sin], axis=-1)
    gamma = 1.0 - jnp.exp2(-5.0 - jnp.arange(N_HEADS, dtype=F32))
    log_g = jnp.log(gamma)
    idx = jnp.arange(CHUNK, dtype=F32)
    diff = idx[:, None] - idx[None, :]
    dmat = jnp.where(diff[None] >= 0,
                     jnp.exp(log_g[:, None, None] * jnp.maximum(diff, 0.0)[None]), 0.0)
    k_decay = jnp.exp(log_g[:, None] * (CHUNK - 1.0 - idx)[None, :])
    q_decay = jnp.exp(log_g[:, None] * (idx + 1.0)[None, :])
    chunk_decay = jnp.exp(log_g * CHUNK)
    widen = lambda t: jnp.repeat(t.T[:, :, None], HEAD_DIM, axis=2).reshape(CHUNK, GROUP_W)
    qdec = widen(q_decay)
    kdec = widen(k_decay)
    cdec = jnp.repeat(chunk_decay[:, None], HEAD_DIM, axis=1).reshape(1, GROUP_W)
    return cos_t, sin_t, dmat, qdec, kdec, cdec


def kernel(x, c, w_ada, b_ada, g_mix, w_in, gmlp_ln_g, gmlp_ln_b, w_spatial, b_spatial,
           ret_norm_g, w_out, g_ffn, w_router, b_router, w1, b1, w2, b2, g_final):
    bsz, seq, d = x.shape
    depth = w_in.shape[0]
    n_tok = bsz * seq
    n_assign = n_tok * TOP_K
    n_slots = n_assign + N_EXPERTS * BM
    n_blocks = n_slots // BM

    mod = _ada_call(c, w_ada, b_ada)
    cos_t, sin_t, dmat, qdec, kdec, cdec = _tables(seq)

    w_in_b = w_in.astype(BF16)
    w_out_b = w_out.astype(BF16)
    wr_t = jnp.swapaxes(w_router, 1, 2).astype(BF16)
    br = b_router.reshape(depth, N_EXPERTS, 1)
    w1_b = w1.astype(BF16).reshape(depth * N_EXPERTS, d, 2 * EXPERT_FF)
    w2_b = w2.astype(BF16).reshape(depth * N_EXPERTS, EXPERT_FF, d)
    b1_r = b1.reshape(depth * N_EXPERTS, 1, 2 * EXPERT_FF)
    b2_r = b2.reshape(depth * N_EXPERTS, 1, d)
    bs_full = jnp.repeat(jnp.swapaxes(b_spatial, 1, 2)[:, :, :, None], HEAD_DIM, axis=3)
    bs_full = bs_full.reshape(depth, CHUNK, GROUP_W)
    row3 = lambda a: a.reshape(depth, 1, a.shape[-1])
    gfin = g_final.reshape(1, d)
    e_ids = jnp.arange(N_EXPERTS, dtype=jnp.int32)

    for l in range(depth):
        mods = [m.reshape(bsz, 1, d) for m in jnp.split(mod[l], N_MOD, axis=-1)]
        shm, scm, gtm, shf, scf, gtf = mods
        x1, h2, idx8, wgt8, rank8, cnt = _mixer_call(
            l, x, (shm, scm, gtm, shf, scf), row3(g_mix), row3(g_ffn), w_in_b,
            row3(gmlp_ln_g), row3(gmlp_ln_b), w_spatial, bs_full, row3(ret_norm_g), w_out_b,
            cos_t, sin_t, dmat, qdec, kdec, cdec, wr_t, br)

        counts = cnt[:, 0].astype(jnp.int32)
        padded = ((counts + BM - 1) // BM) * BM
        pend = jnp.sum(jnp.where(e_ids[None, :] <= e_ids[:, None], padded[None, :], 0), axis=1)
        pstart = pend - padded
        idx = idx8[:TOP_K]
        off = jnp.sum(jnp.where(idx[:, :, None] == e_ids, pstart, 0), axis=-1)
        slot = off + rank8[:TOP_K]
        slot_tiles = slot.reshape(TOP_K, n_tok // TD, TD).transpose(1, 0, 2).reshape(-1)
        blk = jnp.arange(n_blocks, dtype=jnp.int32)
        n_used = pend[-1] // BM
        block_valid = (blk < n_used).astype(jnp.int32)
        block_expert = jnp.minimum(
            jnp.sum((pend[None, :] <= (blk * BM)[:, None]).astype(jnp.int32), axis=1),
            N_EXPERTS - 1)
        block_expert = block_expert + l * N_EXPERTS
        block_src = jnp.minimum(blk, n_used - 1).astype(jnp.int32)

        xs = _dispatch_call(counts, pstart.astype(jnp.int32), (padded).astype(jnp.int32),
                            slot_tiles, h2, n_slots)
        y = _ffn_call(block_expert, block_src, block_valid, xs, w1_b, b1_r, w2_b, b2_r)
        w_tok = wgt8[:TOP_K].T
        x = _combine_call(slot_tiles, w_tok, x1.reshape(n_tok, d), gtf, gfin, y, seq,
                          final=(l == depth - 1)).reshape(bsz, seq, d)
    return x
```

```python
import functools

import jax
import jax.numpy as jnp
from jax import lax
from jax.experimental import pallas as pl
from jax.experimental.pallas import tpu as pltpu

F32 = jnp.float32
BF16 = jnp.bfloat16
I32 = jnp.int32

D_MODEL = 1024
CHUNK = 128
N_HEADS = 4
HEAD_DIM = 128
GROUP_W = N_HEADS * HEAD_DIM
N_EXPERTS = 32
TOP_K = 4
EXPERT_FF = 1024
N_MOD = 6
EPS = 1e-5
ROPE_BASE = 10000.0
SWIGLU_ALPHA = 1.702
SWIGLU_LIMIT = 7.0

TM = 256
BM = 256
GROUP = 8
LOCAL_ROWS = TOP_K * TM + N_EXPERTS * GROUP
TAB = 1024
VMEM_LIMIT = 56 * 1024 * 1024


def _gelu_tanh(x):
    return x * (0.5 * (1.0 + jnp.tanh(0.7978845608028654 * (x + 0.044715 * (x * x * x)))))


def _silu(x):
    return x * jax.nn.sigmoid(x)


def _ada_kernel(c_ref, w_ref, b_ref, o_ref):
    ca = _silu(c_ref[...]).astype(BF16)
    o_ref[0] = jnp.dot(ca, w_ref[0].astype(BF16), preferred_element_type=F32) + b_ref[0]


def _ada_call(c, w_ada, b_ada):
    depth, d, _ = w_ada.shape
    bsz = c.shape[0]
    return pl.pallas_call(
        _ada_kernel,
        grid=(depth, N_MOD),
        in_specs=[
            pl.BlockSpec((bsz, d), lambda l, n: (0, 0)),
            pl.BlockSpec((1, d, d), lambda l, n: (l, 0, n)),
            pl.BlockSpec((1, 1, d), lambda l, n: (l, 0, n)),
        ],
        out_specs=pl.BlockSpec((1, bsz, d), lambda l, n: (l, 0, n)),
        out_shape=jax.ShapeDtypeStruct((depth, bsz, N_MOD * d), F32),
        name="adaln_mod",
    )(c, w_ada, b_ada.reshape(depth, 1, N_MOD * d))


def _mixer_kernel(x_ref, shm_ref, scm_ref, gtm_ref, shf_ref, scf_ref, gmix_ref, gffn_ref,
                  win_ref, lng_ref, lnb_ref, ws_ref, bs_ref, rng_ref, wout_ref,
                  cos_ref, sin_ref, dmat_ref, qdec_ref, kdec_ref, cdec_ref, wr_ref, br_ref,
                  x1_ref, h2_ref, lpos_ref, wgt_ref, tab_ref, cnt_ref,
                  state_ref, base_ref, ob_ref):
    b = pl.program_id(0)
    j = pl.program_id(1)
    n_chunks = TM // CHUNK

    @pl.when(j == 0)
    def _():
        state_ref[...] = jnp.zeros_like(state_ref)

    @pl.when((b == 0) & (j == 0))
    def _():
        base_ref[...] = jnp.zeros_like(base_ref)

    x = x_ref[0]
    ms = jnp.mean(x * x, axis=-1, keepdims=True)
    h = x * lax.rsqrt(ms + EPS) * (gmix_ref[0] * (1.0 + scm_ref[0])) + shm_ref[0]
    hb = h.astype(BF16)

    def proj(k):
        return jnp.dot(hb, win_ref[0, :, k * GROUP_W:(k + 1) * GROUP_W],
                       preferred_element_type=F32)

    def rows(c):
        return slice(c * CHUNK, (c + 1) * CHUNK)

    def lanes(hh):
        return slice(hh * HEAD_DIM, (hh + 1) * HEAD_DIM)

    gu = _gelu_tanh(proj(0))
    gv = _gelu_tanh(proj(1))
    mu = jnp.mean(gv, axis=-1, keepdims=True)
    vc = gv - mu
    var = jnp.mean(vc * vc, axis=-1, keepdims=True)
    vn = (vc * lax.rsqrt(var + EPS) * lng_ref[0] + lnb_ref[0]).astype(BF16)
    t_io = lax.broadcasted_iota(I32, (CHUNK, CHUNK), 0)
    s_io = lax.broadcasted_iota(I32, (CHUNK, CHUNK), 1)
    causal = t_io >= s_io
    bias = bs_ref[0]
    for hh in range(N_HEADS):
        wm = jnp.where(causal, ws_ref[0, hh], 0.0).astype(BF16)
        for c in range(n_chunks):
            mixed = jnp.dot(wm, vn[rows(c), lanes(hh)], preferred_element_type=F32)
            oa = gu[rows(c), lanes(hh)] * (mixed + bias[:, lanes(hh)])
            ob_ref[rows(c), lanes(hh)] = oa.astype(BF16)

    q = proj(2)
    k = proj(3)
    vr = proj(4).astype(BF16)
    g = proj(5)
    cos = cos_ref[...]
    sin = sin_ref[...]
    k_scale = HEAD_DIM ** -0.5
    for hh in range(N_HEADS):
        qh = q[:, lanes(hh)]
        kh = k[:, lanes(hh)]
        qh = qh * cos + pltpu.roll(qh, HEAD_DIM // 2, 1) * sin
        kh = (kh * cos + pltpu.roll(kh, HEAD_DIM // 2, 1) * sin) * k_scale
        dm = dmat_ref[hh]
        qd = qdec_ref[:, lanes(hh)]
        kd = kdec_ref[:, lanes(hh)]
        cd = cdec_ref[:, lanes(hh)]
        gn = rng_ref[0][:, lanes(hh)]
        for c in range(n_chunks):
            qc = qh[rows(c)]
            kc = kh[rows(c)]
            vcb = vr[rows(c), lanes(hh)]
            s = lax.dot_general(qc.astype(BF16), kc.astype(BF16),
                                (((1,), (1,)), ((), ())), preferred_element_type=F32) * dm
            st = state_ref[hh]
            o = jnp.dot(s.astype(BF16), vcb, preferred_element_type=F32)
            o = o + jnp.dot((qc * qd).astype(BF16), st.astype(BF16), preferred_element_type=F32)
            kdt = (kc * kd).T.astype(BF16)
            state_ref[hh] = st * cd + jnp.dot(kdt, vcb, preferred_element_type=F32)
            o = o * lax.rsqrt(jnp.mean(o * o, axis=-1, keepdims=True) + EPS) * gn
            ob = o * _silu(g[rows(c), lanes(hh)])
            ob_ref[rows(c), GROUP_W + hh * HEAD_DIM:GROUP_W + (hh + 1) * HEAD_DIM] = ob.astype(BF16)

    y = jnp.dot(ob_ref[...], wout_ref[0], preferred_element_type=F32)
    x1 = x + gtm_ref[0] * y
    x1_ref[0] = x1

    ms2 = jnp.mean(x1 * x1, axis=-1, keepdims=True)
    h2 = x1 * lax.rsqrt(ms2 + EPS) * (gffn_ref[0] * (1.0 + scf_ref[0])) + shf_ref[0]
    h2b = h2.astype(BF16)
    h2_ref[...] = h2b

    lg = lax.dot_general(wr_ref[0], h2b, (((1,), (1,)), ((), ())),
                         preferred_element_type=F32) + br_ref[0]
    e_io = lax.broadcasted_iota(I32, (N_EXPERTS, TM), 0)
    work = lg
    vals, ids = [], []
    for _ in range(TOP_K):
        m = jnp.max(work, axis=0, keepdims=True)
        ik = jnp.min(jnp.where(work == m, e_io, N_EXPERTS), axis=0, keepdims=True)
        vals.append(m)
        ids.append(ik)
        work = jnp.where(e_io == ik, -jnp.inf, work)
    exps = [jnp.exp(v - vals[0]) for v in vals]
    denom = exps[0] + exps[1] + exps[2] + exps[3]
    wts = [e / denom for e in exps]

    sel = [e_io == ik for ik in ids]
    oh = jnp.where(sel[0] | sel[1] | sel[2] | sel[3], 1.0, 0.0)
    r_io = lax.broadcasted_iota(I32, (TM, TM), 0)
    c_io = lax.broadcasted_iota(I32, (TM, TM), 1)
    upper = jnp.where(r_io < c_io, 1.0, 0.0).astype(BF16)
    cum = jnp.dot(oh.astype(BF16), upper, preferred_element_type=F32)
    cnt_t = jnp.sum(oh, axis=1, keepdims=True)
    run = jnp.ceil(cnt_t * (1.0 / GROUP)) * GROUP
    run_b = jnp.broadcast_to(run, (N_EXPERTS, TM))
    er_io = lax.broadcasted_iota(I32, (N_EXPERTS, N_EXPERTS), 0)
    ec_io = lax.broadcasted_iota(I32, (N_EXPERTS, N_EXPERTS), 1)
    lower = jnp.where(ec_io < er_io, 1.0, 0.0).astype(BF16)
    lstart = jnp.dot(lower, run_b.astype(BF16), preferred_element_type=F32)
    pos = lstart + cum
    lpos = [jnp.sum(jnp.where(sk, pos, 0.0), axis=0, keepdims=True) for sk in sel]
    gbase = base_ref[...]
    base_ref[...] = gbase + run_b
    cnt_ref[...] = base_ref[:, :128]
    tab_ref[0, 0] = gbase[:, :128].astype(I32)
    tab_ref[0, 1] = run_b[:, :128].astype(I32)
    tab_ref[0, 2] = lstart[:, :128].astype(I32)

    row8 = lax.broadcasted_iota(I32, (8, TM), 0)

    def stack4(parts, fill):
        out = jnp.full((8, TM), fill, parts[0].dtype)
        for kk in range(TOP_K):
            out = jnp.where(row8 == kk, parts[kk], out)
        return out

    lpos_ref[...] = stack4([p.astype(I32) for p in lpos], 0)
    wgt_ref[...] = stack4(wts, 0.0)


def _mixer_call(layer, x, mods, g_mix, g_ffn, w_in_b, ln_g, ln_b, w_sp, bs_full, rn_g,
                w_out_b, cos_t, sin_t, dmat, qdec, kdec, cdec, wr_t, br):
    bsz, seq, d = x.shape
    n_tok = bsz * seq
    nj = seq // TM
    shm, scm, gtm, shf, scf = mods
    vec = lambda: pl.BlockSpec((1, 1, d), lambda b, j: (b, 0, 0))
    lvec = lambda w: pl.BlockSpec((1, 1, w), lambda b, j: (layer, 0, 0))
    full = lambda shp: pl.BlockSpec(shp, lambda b, j: (0,) * len(shp))
    tok = lambda shp: pl.BlockSpec(shp, lambda b, j: (0, b * nj + j))
    in_specs = [
        pl.BlockSpec((1, TM, d), lambda b, j: (b, j, 0)),
        vec(), vec(), vec(), vec(), vec(),
        lvec(d), lvec(d),
        pl.BlockSpec((1, d, 6 * GROUP_W), lambda b, j: (layer, 0, 0)),
        lvec(GROUP_W), lvec(GROUP_W),
        pl.BlockSpec((1, N_HEADS, CHUNK, CHUNK), lambda b, j: (layer, 0, 0, 0)),
        pl.BlockSpec((1, CHUNK, GROUP_W), lambda b, j: (layer, 0, 0)),
        lvec(GROUP_W),
        pl.BlockSpec((1, d, d), lambda b, j: (layer, 0, 0)),
        pl.BlockSpec((TM, HEAD_DIM), lambda b, j: (j, 0)),
        pl.BlockSpec((TM, HEAD_DIM), lambda b, j: (j, 0)),
        full((N_HEADS, CHUNK, CHUNK)),
        full((CHUNK, GROUP_W)), full((CHUNK, GROUP_W)), full((1, GROUP_W)),
        pl.BlockSpec((1, N_EXPERTS, d), lambda b, j: (layer, 0, 0)),
        pl.BlockSpec((1, N_EXPERTS, 1), lambda b, j: (layer, 0, 0)),
    ]
    out_specs = [
        pl.BlockSpec((1, TM, d), lambda b, j: (b, j, 0)),
        pl.BlockSpec((TM, d), lambda b, j: (b * nj + j, 0)),
        tok((8, TM)), tok((8, TM)),
        pl.BlockSpec((1, 3, N_EXPERTS, 128), lambda b, j: (b * nj + j, 0, 0, 0)),
        pl.BlockSpec((N_EXPERTS, 128), lambda b, j: (0, 0)),
    ]
    out_shape = [
        jax.ShapeDtypeStruct((bsz, seq, d), F32),
        jax.ShapeDtypeStruct((n_tok, d), BF16),
        jax.ShapeDtypeStruct((8, n_tok), I32),
        jax.ShapeDtypeStruct((8, n_tok), F32),
        jax.ShapeDtypeStruct((bsz * nj, 3, N_EXPERTS, 128), I32),
        jax.ShapeDtypeStruct((N_EXPERTS, 128), F32),
    ]
    return pl.pallas_call(
        _mixer_kernel,
        grid=(bsz, nj),
        in_specs=in_specs,
        out_specs=out_specs,
        out_shape=out_shape,
        scratch_shapes=[
            pltpu.VMEM((N_HEADS, HEAD_DIM, HEAD_DIM), F32),
            pltpu.VMEM((N_EXPERTS, TM), F32),
            pltpu.VMEM((TM, d), BF16),
        ],
        compiler_params=pltpu.CompilerParams(
            dimension_semantics=("arbitrary", "arbitrary"), vmem_limit_bytes=VMEM_LIMIT),
        name="mixer_router",
    )(x, shm, scm, gtm, shf, scf, g_mix, g_ffn, w_in_b, ln_g, ln_b, w_sp, bs_full, rn_g,
      w_out_b, cos_t, sin_t, dmat, qdec, kdec, cdec, wr_t, br)


def _run_copies(tab_ref, make_copy):
    def per_expert(e, carry):
        slot0 = tab_ref[e]
        local0 = tab_ref[2 * N_EXPERTS + e]

        def piece(p, c2):
            off = p * GROUP
            make_copy(pl.multiple_of(local0 + off, GROUP), pl.multiple_of(slot0 + off, GROUP)).start()
            return c2

        lax.fori_loop(0, tab_ref[N_EXPERTS + e], piece, 0)
        return carry

    lax.fori_loop(0, N_EXPERTS, per_expert, 0)


def _wait_copies(tab_ref, one_copy):
    def body(p, c):
        one_copy.wait()
        return c

    lax.fori_loop(0, tab_ref[3 * N_EXPERTS], body, 0)


def _dispatch_kernel(cntp_ref, pstart_ref, padded_ref, tab_ref, lpos_ref, h2_ref, xs_hbm,
                     xl_ref, zblk, sem, zsem):
    i = pl.program_id(0)
    lp = lpos_ref[...]
    r_io = lax.broadcasted_iota(I32, (LOCAL_ROWS, TM), 0)
    hit = r_io == lp[0:1]
    for kk in range(1, TOP_K):
        hit = hit | (r_io == lp[kk:kk + 1])
    perm = jnp.where(hit, 1.0, 0.0).astype(BF16)
    xl_ref[...] = jnp.dot(perm, h2_ref[...], preferred_element_type=F32)

    def make_copy(local_row, slot_row):
        return pltpu.make_async_copy(xl_ref.at[pl.ds(local_row, GROUP)],
                                     xs_hbm.at[pl.ds(slot_row, GROUP)], sem)

    _run_copies(tab_ref, make_copy)

    @pl.when(i == 0)
    def _():
        zblk[...] = jnp.zeros_like(zblk)

        def per_expert(e, carry):
            start = pstart_ref[e]

            def zcopy(r):
                return pltpu.make_async_copy(
                    zblk.at[pl.ds(0, GROUP)],
                    xs_hbm.at[pl.ds(pl.multiple_of(start + r * GROUP, GROUP), GROUP)], zsem)

            def fill(r, c2):
                zcopy(r).start()
                return c2

            def drain(r, c2):
                zcopy(r).wait()
                return c2

            lax.fori_loop(cntp_ref[e] // GROUP, padded_ref[e] // GROUP, fill, 0)
            lax.fori_loop(cntp_ref[e] // GROUP, padded_ref[e] // GROUP, drain, 0)
            return carry

        lax.fori_loop(0, N_EXPERTS, per_expert, 0)

        first_tail = (pstart_ref[N_EXPERTS - 1] + padded_ref[N_EXPERTS - 1]) // BM

        def tail(blk, carry):
            cp = pltpu.make_async_copy(
                zblk, xs_hbm.at[pl.ds(pl.multiple_of(blk * BM, BM), BM)], zsem)
            cp.start()
            cp.wait()
            return carry

        lax.fori_loop(first_tail, xs_hbm.shape[0] // BM, tail, 0)

    _wait_copies(tab_ref, make_copy(0, 0))


def _dispatch_call(cntp, pstart, padded, tab, lpos8, h2, n_slots):
    n_tok, d = h2.shape
    grid_spec = pltpu.PrefetchScalarGridSpec(
        num_scalar_prefetch=3,
        grid=(n_tok // TM,),
        in_specs=[
            pl.BlockSpec((TAB,), lambda i, *_: (i,), memory_space=pltpu.SMEM),
            pl.BlockSpec((8, TM), lambda i, *_: (0, i)),
            pl.BlockSpec((TM, d), lambda i, *_: (i, 0)),
        ],
        out_specs=pl.BlockSpec(memory_space=pl.ANY),
        scratch_shapes=[pltpu.VMEM((LOCAL_ROWS, d), F32), pltpu.VMEM((BM, d), F32),
                        pltpu.SemaphoreType.DMA, pltpu.SemaphoreType.DMA],
    )
    return pl.pallas_call(
        _dispatch_kernel,
        grid_spec=grid_spec,
        out_shape=jax.ShapeDtypeStruct((n_slots, d), F32),
        compiler_params=pltpu.CompilerParams(
            dimension_semantics=("arbitrary",), vmem_limit_bytes=VMEM_LIMIT),
        name="moe_dispatch",
    )(cntp, pstart, padded, tab, lpos8, h2)


def _ffn_kernel(be_ref, bsrc_ref, bvalid_ref, xs_ref, w1_ref, b1_ref, w2_ref, b2_ref, y_ref):
    i = pl.program_id(0)

    @pl.when(bvalid_ref[i] == 1)
    def _():
        a = jnp.dot(xs_ref[...].astype(BF16), w1_ref[0], preferred_element_type=F32) + b1_ref[0]
        a_glu = jnp.minimum(a[:, :EXPERT_FF], SWIGLU_LIMIT)
        a_lin = jnp.clip(a[:, EXPERT_FF:], -SWIGLU_LIMIT, SWIGLU_LIMIT)
        act = a_glu * jax.nn.sigmoid(SWIGLU_ALPHA * a_glu) * (a_lin + 1.0)
        y_ref[...] = jnp.dot(act.astype(BF16), w2_ref[0], preferred_element_type=F32) + b2_ref[0]

    @pl.when(bvalid_ref[i] == 0)
    def _():
        y_ref[...] = jnp.zeros_like(y_ref)


def _ffn_call(block_expert, block_src, block_valid, xs, w1_b, b1, w2_b, b2):
    n_slots, d = xs.shape
    n_blocks = n_slots // BM
    grid_spec = pltpu.PrefetchScalarGridSpec(
        num_scalar_prefetch=3,
        grid=(n_blocks,),
        in_specs=[
            pl.BlockSpec((BM, d), lambda i, be, bs, bv: (bs[i], 0)),
            pl.BlockSpec((1, d, 2 * EXPERT_FF), lambda i, be, bs, bv: (be[i], 0, 0)),
            pl.BlockSpec((1, 1, 2 * EXPERT_FF), lambda i, be, bs, bv: (be[i], 0, 0)),
            pl.BlockSpec((1, EXPERT_FF, d), lambda i, be, bs, bv: (be[i], 0, 0)),
            pl.BlockSpec((1, 1, d), lambda i, be, bs, bv: (be[i], 0, 0)),
        ],
        out_specs=pl.BlockSpec((BM, d), lambda i, be, bs, bv: (i, 0)),
    )
    return pl.pallas_call(
        _ffn_kernel,
        grid_spec=grid_spec,
        out_shape=jax.ShapeDtypeStruct((n_slots, d), F32),
        compiler_params=pltpu.CompilerParams(
            dimension_semantics=("arbitrary",), vmem_limit_bytes=VMEM_LIMIT),
        name="moe_ffn",
    )(block_expert, block_src, block_valid, xs, w1_b, b1, w2_b, b2)


def _combine_kernel(tab_ref, lpos_ref, w_ref, x1_ref, gt_ref, gfin_ref, y_hbm, o_ref,
                    yl_ref, sem, *, final):
    @pl.when(pl.program_id(0) == 0)
    def _():
        yl_ref[...] = jnp.zeros_like(yl_ref)

    def make_copy(local_row, slot_row):
        return pltpu.make_async_copy(y_hbm.at[pl.ds(slot_row, GROUP)],
                                     yl_ref.at[pl.ds(local_row, GROUP)], sem)

    _run_copies(tab_ref, make_copy)
    lp = lpos_ref[...]
    w = w_ref[...]
    l_io = lax.broadcasted_iota(I32, (TM, LOCAL_ROWS), 1)
    unperm = jnp.where(l_io == lp[:, 0:1], w[:, 0:1], 0.0)
    for kk in range(1, TOP_K):
        unperm = unperm + jnp.where(l_io == lp[:, kk:kk + 1], w[:, kk:kk + 1], 0.0)
    _wait_copies(tab_ref, make_copy(0, 0))
    acc = jnp.dot(unperm.astype(BF16), yl_ref[...].astype(BF16), preferred_element_type=F32)
    out = x1_ref[...] + gt_ref[0] * acc
    if final:
        out = out * lax.rsqrt(jnp.mean(out * out, axis=-1, keepdims=True) + EPS) * gfin_ref[...]
    o_ref[...] = out


def _combine_call(tab, lpos_tok, w_tok, x1, gtf, g_final, y, seq, final):
    n_tok, d = x1.shape
    per_b = seq // TM
    return pl.pallas_call(
        functools.partial(_combine_kernel, final=final),
        grid=(n_tok // TM,),
        in_specs=[
            pl.BlockSpec((TAB,), lambda i: (i,), memory_space=pltpu.SMEM),
            pl.BlockSpec((TM, TOP_K), lambda i: (i, 0)),
            pl.BlockSpec((TM, TOP_K), lambda i: (i, 0)),
            pl.BlockSpec((TM, d), lambda i: (i, 0)),
            pl.BlockSpec((1, 1, d), lambda i: (i // per_b, 0, 0)),
            pl.BlockSpec((1, d), lambda i: (0, 0)),
            pl.BlockSpec(memory_space=pl.ANY),
        ],
        out_specs=pl.BlockSpec((TM, d), lambda i: (i, 0)),
        out_shape=jax.ShapeDtypeStruct((n_tok, d), F32),
        scratch_shapes=[pltpu.VMEM((LOCAL_ROWS, d), F32), pltpu.SemaphoreType.DMA],
        compiler_params=pltpu.CompilerParams(
            dimension_semantics=("arbitrary",), vmem_limit_bytes=VMEM_LIMIT),
        name="moe_combine",
    )(tab, lpos_tok, w_tok, x1, gtf, g_final, y)


def _tables(seq):
    half = HEAD_DIM // 2
    inv_freq = ROPE_BASE ** (-jnp.arange(half, dtype=F32) / half)
    ang = jnp.arange(seq, dtype=F32)[:, None] * inv_freq[None, :]
    cos = jnp.cos(ang)
    sin = jnp.sin(ang)
    cos_t = jnp.concatenate([cos, cos], axis=-1)
    sin_t = jnp.concatenate([-sin, sin], axis=-1)
    gamma = 1.0 - jnp.exp2(-5.0 - jnp.arange(N_HEADS, dtype=F32))
    log_g = jnp.log(gamma)
    idx = jnp.arange(CHUNK, dtype=F32)
    diff = idx[:, None] - idx[None, :]
    dmat = jnp.where(diff[None] >= 0,
                     jnp.exp(log_g[:, None, None] * jnp.maximum(diff, 0.0)[None]), 0.0)
    k_decay = jnp.exp(log_g[:, None] * (CHUNK - 1.0 - idx)[None, :])
    q_decay = jnp.exp(log_g[:, None] * (idx + 1.0)[None, :])
    chunk_decay = jnp.exp(log_g * CHUNK)
    widen = lambda t: jnp.repeat(t.T[:, :, None], HEAD_DIM, axis=2).reshape(CHUNK, GROUP_W)
    qdec = widen(q_decay)
    kdec = widen(k_decay)
    cdec = jnp.repeat(chunk_decay[:, None], HEAD_DIM, axis=1).reshape(1, GROUP_W)
    return cos_t, sin_t, dmat, qdec, kdec, cdec


def kernel(x, c, w_ada, b_ada, g_mix, w_in, gmlp_ln_g, gmlp_ln_b, w_spatial, b_spatial,
           ret_norm_g, w_out, g_ffn, w_router, b_router, w1, b1, w2, b2, g_final):
    bsz, seq, d = x.shape
    depth = w_in.shape[0]
    n_tok = bsz * seq
    n_tiles = n_tok // TM
    n_slots = n_tok * TOP_K + n_tiles * N_EXPERTS * GROUP + N_EXPERTS * BM
    assert n_slots % BM == 0
    n_blocks = n_slots // BM

    mod = _ada_call(c, w_ada, b_ada)
    cos_t, sin_t, dmat, qdec, kdec, cdec = _tables(seq)

    w_in_b = w_in.astype(BF16)
    w_out_b = w_out.astype(BF16)
    wr_t = jnp.swapaxes(w_router, 1, 2).astype(BF16)
    br = b_router.reshape(depth, N_EXPERTS, 1)
    w1_b = w1.astype(BF16).reshape(depth * N_EXPERTS, d, 2 * EXPERT_FF)
    w2_b = w2.astype(BF16).reshape(depth * N_EXPERTS, EXPERT_FF, d)
    b1_r = b1.reshape(depth * N_EXPERTS, 1, 2 * EXPERT_FF)
    b2_r = b2.reshape(depth * N_EXPERTS, 1, d)
    bs_full = jnp.repeat(jnp.swapaxes(b_spatial, 1, 2)[:, :, :, None], HEAD_DIM, axis=3)
    bs_full = bs_full.reshape(depth, CHUNK, GROUP_W)
    row3 = lambda a: a.reshape(depth, 1, a.shape[-1])
    gfin = g_final.reshape(1, d)
    e_ids = jnp.arange(N_EXPERTS, dtype=I32)
    blk = jnp.arange(n_blocks, dtype=I32)

    for l in range(depth):
        mods = [m.reshape(bsz, 1, d) for m in jnp.split(mod[l], N_MOD, axis=-1)]
        shm, scm, gtm, shf, scf, gtf = mods
        x1, h2, lpos8, wgt8, tab3, cnt = _mixer_call(
            l, x, (shm, scm, gtm, shf, scf), row3(g_mix), row3(g_ffn), w_in_b,
            row3(gmlp_ln_g), row3(gmlp_ln_b), w_spatial, bs_full, row3(ret_norm_g), w_out_b,
            cos_t, sin_t, dmat, qdec, kdec, cdec, wr_t, br)

        cntp = cnt[:, 0].astype(I32)
        padded = ((cntp + BM - 1) // BM) * BM
        pend = jnp.sum(jnp.where(e_ids[None, :] <= e_ids[:, None], padded[None, :], 0), axis=1)
        pstart = pend - padded
        gbase, runs, lstart = tab3[:, 0, :, 0], tab3[:, 1, :, 0], tab3[:, 2, :, 0]
        pieces = runs // GROUP
        tab = jnp.concatenate(
            [pstart[None, :] + gbase, pieces, lstart, jnp.sum(pieces, axis=1, keepdims=True),
             jnp.zeros((n_tiles, TAB - 3 * N_EXPERTS - 1), I32)], axis=1).reshape(-1)
        n_used = pend[-1] // BM
        block_valid = (blk < n_used).astype(I32)
        block_expert = jnp.minimum(
            jnp.sum((pend[None, :] <= (blk * BM)[:, None]).astype(I32), axis=1),
            N_EXPERTS - 1) + l * N_EXPERTS
        block_src = jnp.minimum(blk, n_used - 1).astype(I32)

        xs = _dispatch_call(cntp, pstart, padded, tab, lpos8, h2, n_slots)
        y = _ffn_call(block_expert, block_src, block_valid, xs, w1_b, b1_r, w2_b, b2_r)
        x = _combine_call(tab, lpos8[:TOP_K].T, wgt8[:TOP_K].T, x1.reshape(n_tok, d), gtf, gfin,
                          y, seq, final=(l == depth - 1)).reshape(bsz, seq, d)
    return x
```

```python
import functools

import jax
import jax.numpy as jnp
from jax import lax
from jax.experimental import pallas as pl
from jax.experimental.pallas import tpu as pltpu

F32 = jnp.float32
BF16 = jnp.bfloat16
I32 = jnp.int32

D_MODEL = 1024
CHUNK = 128
N_HEADS = 4
HEAD_DIM = 128
GROUP_W = N_HEADS * HEAD_DIM
N_EXPERTS = 32
TOP_K = 4
EXPERT_FF = 1024
N_MOD = 6
EPS = 1e-5
ROPE_BASE = 10000.0
SWIGLU_ALPHA = 1.702
SWIGLU_LIMIT = 7.0

TM = 256
BM = 512
GROUP = 8
LOCAL_ROWS = TOP_K * TM + N_EXPERTS * GROUP
TAB = 1024
VMEM_LIMIT = 56 * 1024 * 1024


def _gelu_tanh(x):
    return x * (0.5 * (1.0 + jnp.tanh(0.7978845608028654 * (x + 0.044715 * (x * x * x)))))


def _silu(x):
    return x * jax.nn.sigmoid(x)


def _ada_kernel(c_ref, w_ref, b_ref, o_ref):
    ca = _silu(c_ref[...]).astype(BF16)
    o_ref[0] = jnp.dot(ca, w_ref[0].astype(BF16), preferred_element_type=F32) + b_ref[0]


def _ada_call(c, w_ada, b_ada):
    depth, d, _ = w_ada.shape
    bsz = c.shape[0]
    return pl.pallas_call(
        _ada_kernel,
        grid=(depth, N_MOD),
        in_specs=[
            pl.BlockSpec((bsz, d), lambda l, n: (0, 0)),
            pl.BlockSpec((1, d, d), lambda l, n: (l, 0, n)),
            pl.BlockSpec((1, 1, d), lambda l, n: (l, 0, n)),
        ],
        out_specs=pl.BlockSpec((1, bsz, d), lambda l, n: (l, 0, n)),
        out_shape=jax.ShapeDtypeStruct((depth, bsz, N_MOD * d), F32),
        name="adaln_mod",
    )(c, w_ada, b_ada.reshape(depth, 1, N_MOD * d))


def _mixer_kernel(x_ref, shm_ref, scm_ref, gtm_ref, shf_ref, scf_ref, gmix_ref, gffn_ref,
                  win_ref, lng_ref, lnb_ref, ws_ref, bs_ref, rng_ref, wout_ref,
                  cos_ref, sin_ref, dmat_ref, qdec_ref, kdec_ref, cdec_ref, wr_ref, br_ref,
                  x1_ref, h2_ref, lpos_ref, wgt_ref, tab_ref, cnt_ref,
                  state_ref, base_ref, ob_ref):
    b = pl.program_id(0)
    j = pl.program_id(1)
    n_chunks = TM // CHUNK

    @pl.when(j == 0)
    def _():
        state_ref[...] = jnp.zeros_like(state_ref)

    @pl.when((b == 0) & (j == 0))
    def _():
        base_ref[...] = jnp.zeros_like(base_ref)

    x = x_ref[0]
    ms = jnp.mean(x * x, axis=-1, keepdims=True)
    h = x * lax.rsqrt(ms + EPS) * (gmix_ref[0] * (1.0 + scm_ref[0])) + shm_ref[0]
    hb = h.astype(BF16)

    def proj(k):
        return jnp.dot(hb, win_ref[0, :, k * GROUP_W:(k + 1) * GROUP_W],
                       preferred_element_type=F32)

    def rows(c):
        return slice(c * CHUNK, (c + 1) * CHUNK)

    def lanes(hh):
        return slice(hh * HEAD_DIM, (hh + 1) * HEAD_DIM)

    gu = _gelu_tanh(proj(0))
    gv = _gelu_tanh(proj(1))
    mu = jnp.mean(gv, axis=-1, keepdims=True)
    vc = gv - mu
    var = jnp.mean(vc * vc, axis=-1, keepdims=True)
    vn = (vc * lax.rsqrt(var + EPS) * lng_ref[0] + lnb_ref[0]).astype(BF16)
    t_io = lax.broadcasted_iota(I32, (CHUNK, CHUNK), 0)
    s_io = lax.broadcasted_iota(I32, (CHUNK, CHUNK), 1)
    causal = t_io >= s_io
    bias = bs_ref[0]
    for hh in range(N_HEADS):
        wm = jnp.where(causal, ws_ref[0, hh], 0.0).astype(BF16)
        for c in range(n_chunks):
            mixed = jnp.dot(wm, vn[rows(c), lanes(hh)], preferred_element_type=F32)
            oa = gu[rows(c), lanes(hh)] * (mixed + bias[:, lanes(hh)])
            ob_ref[rows(c), lanes(hh)] = oa.astype(BF16)

    q = proj(2)
    k = proj(3)
    vr = proj(4).astype(BF16)
    g = proj(5)
    cos = cos_ref[...]
    sin = sin_ref[...]
    k_scale = HEAD_DIM ** -0.5
    for hh in range(N_HEADS):
        qh = q[:, lanes(hh)]
        kh = k[:, lanes(hh)]
        qh = qh * cos + pltpu.roll(qh, HEAD_DIM // 2, 1) * sin
        kh = (kh * cos + pltpu.roll(kh, HEAD_DIM // 2, 1) * sin) * k_scale
        dm = dmat_ref[hh]
        qd = qdec_ref[:, lanes(hh)]
        kd = kdec_ref[:, lanes(hh)]
        cd = cdec_ref[:, lanes(hh)]
        gn = rng_ref[0][:, lanes(hh)]
        for c in range(n_chunks):
            qc = qh[rows(c)]
            kc = kh[rows(c)]
            vcb = vr[rows(c), lanes(hh)]
            s = lax.dot_general(qc.astype(BF16), kc.astype(BF16),
                                (((1,), (1,)), ((), ())), preferred_element_type=F32) * dm
            st = state_ref[hh]
            o = jnp.dot(s.astype(BF16), vcb, preferred_element_type=F32)
            o = o + jnp.dot((qc * qd).astype(BF16), st.astype(BF16), preferred_element_type=F32)
            kdt = (kc * kd).T.astype(BF16)
            state_ref[hh] = st * cd + jnp.dot(kdt, vcb, preferred_element_type=F32)
            o = o * lax.rsqrt(jnp.mean(o * o, axis=-1, keepdims=True) + EPS) * gn
            ob = o * _silu(g[rows(c), lanes(hh)])
            ob_ref[rows(c), GROUP_W + hh * HEAD_DIM:GROUP_W + (hh + 1) * HEAD_DIM] = ob.astype(BF16)

    y = jnp.dot(ob_ref[...], wout_ref[0], preferred_element_type=F32)
    x1 = x + gtm_ref[0] * y
    x1_ref[0] = x1

    ms2 = jnp.mean(x1 * x1, axis=-1, keepdims=True)
    h2 = x1 * lax.rsqrt(ms2 + EPS) * (gffn_ref[0] * (1.0 + scf_ref[0])) + shf_ref[0]
    h2b = h2.astype(BF16)
    h2_ref[...] = h2b

    lg = lax.dot_general(wr_ref[0], h2b, (((1,), (1,)), ((), ())),
                         preferred_element_type=F32) + br_ref[0]
    e_io = lax.broadcasted_iota(I32, (N_EXPERTS, TM), 0)
    work = lg
    vals, ids = [], []
    for _ in range(TOP_K):
        m = jnp.max(work, axis=0, keepdims=True)
        ik = jnp.min(jnp.where(work == m, e_io, N_EXPERTS), axis=0, keepdims=True)
        vals.append(m)
        ids.append(ik)
        work = jnp.where(e_io == ik, -jnp.inf, work)
    exps = [jnp.exp(v - vals[0]) for v in vals]
    denom = exps[0] + exps[1] + exps[2] + exps[3]
    wts = [e / denom for e in exps]

    sel = [e_io == ik for ik in ids]
    oh = jnp.where(sel[0] | sel[1] | sel[2] | sel[3], 1.0, 0.0)
    r_io = lax.broadcasted_iota(I32, (TM, TM), 0)
    c_io = lax.broadcasted_iota(I32, (TM, TM), 1)
    upper = jnp.where(r_io < c_io, 1.0, 0.0).astype(BF16)
    cum = jnp.dot(oh.astype(BF16), upper, preferred_element_type=F32)
    cnt_t = jnp.sum(oh, axis=1, keepdims=True)
    run = jnp.ceil(cnt_t * (1.0 / GROUP)) * GROUP
    run_b = jnp.broadcast_to(run, (N_EXPERTS, TM))
    er_io = lax.broadcasted_iota(I32, (N_EXPERTS, N_EXPERTS), 0)
    ec_io = lax.broadcasted_iota(I32, (N_EXPERTS, N_EXPERTS), 1)
    lower = jnp.where(ec_io < er_io, 1.0, 0.0).astype(BF16)
    lstart = jnp.dot(lower, run_b.astype(BF16), preferred_element_type=F32)
    pos = lstart + cum
    lpos = [jnp.sum(jnp.where(sk, pos, 0.0), axis=0, keepdims=True) for sk in sel]
    gbase = base_ref[...]
    base_ref[...] = gbase + run_b
    cnt_ref[...] = base_ref[:, :128]
    tab_ref[0, 0] = gbase[:, :128].astype(I32)
    tab_ref[0, 1] = run_b[:, :128].astype(I32)
    tab_ref[0, 2] = lstart[:, :128].astype(I32)

    row8 = lax.broadcasted_iota(I32, (8, TM), 0)

    def stack4(parts, fill):
        out = jnp.full((8, TM), fill, parts[0].dtype)
        for kk in range(TOP_K):
            out = jnp.where(row8 == kk, parts[kk], out)
        return out

    lpos_ref[...] = stack4([p.astype(I32) for p in lpos], 0)
    wgt_ref[...] = stack4(wts, 0.0)


def _mixer_call(layer, x, mods, g_mix, g_ffn, w_in_b, ln_g, ln_b, w_sp, bs_full, rn_g,
                w_out_b, cos_t, sin_t, dmat, qdec, kdec, cdec, wr_t, br):
    bsz, seq, d = x.shape
    n_tok = bsz * seq
    nj = seq // TM
    shm, scm, gtm, shf, scf = mods
    vec = lambda: pl.BlockSpec((1, 1, d), lambda b, j: (b, 0, 0))
    lvec = lambda w: pl.BlockSpec((1, 1, w), lambda b, j: (layer, 0, 0))
    full = lambda shp: pl.BlockSpec(shp, lambda b, j: (0,) * len(shp))
    tok = lambda shp: pl.BlockSpec(shp, lambda b, j: (0, b * nj + j))
    in_specs = [
        pl.BlockSpec((1, TM, d), lambda b, j: (b, j, 0)),
        vec(), vec(), vec(), vec(), vec(),
        lvec(d), lvec(d),
        pl.BlockSpec((1, d, 6 * GROUP_W), lambda b, j: (layer, 0, 0)),
        lvec(GROUP_W), lvec(GROUP_W),
        pl.BlockSpec((1, N_HEADS, CHUNK, CHUNK), lambda b, j: (layer, 0, 0, 0)),
        pl.BlockSpec((1, CHUNK, GROUP_W), lambda b, j: (layer, 0, 0)),
        lvec(GROUP_W),
        pl.BlockSpec((1, d, d), lambda b, j: (layer, 0, 0)),
        pl.BlockSpec((TM, HEAD_DIM), lambda b, j: (j, 0)),
        pl.BlockSpec((TM, HEAD_DIM), lambda b, j: (j, 0)),
        full((N_HEADS, CHUNK, CHUNK)),
        full((CHUNK, GROUP_W)), full((CHUNK, GROUP_W)), full((1, GROUP_W)),
        pl.BlockSpec((1, N_EXPERTS, d), lambda b, j: (layer, 0, 0)),
        pl.BlockSpec((1, N_EXPERTS, 1), lambda b, j: (layer, 0, 0)),
    ]
    out_specs = [
        pl.BlockSpec((1, TM, d), lambda b, j: (b, j, 0)),
        pl.BlockSpec((TM, d), lambda b, j: (b * nj + j, 0)),
        tok((8, TM)), tok((8, TM)),
        pl.BlockSpec((1, 3, N_EXPERTS, 128), lambda b, j: (b * nj + j, 0, 0, 0)),
        pl.BlockSpec((N_EXPERTS, 128), lambda b, j: (0, 0)),
    ]
    out_shape = [
        jax.ShapeDtypeStruct((bsz, seq, d), F32),
        jax.ShapeDtypeStruct((n_tok, d), BF16),
        jax.ShapeDtypeStruct((8, n_tok), I32),
        jax.ShapeDtypeStruct((8, n_tok), F32),
        jax.ShapeDtypeStruct((bsz * nj, 3, N_EXPERTS, 128), I32),
        jax.ShapeDtypeStruct((N_EXPERTS, 128), F32),
    ]
    return pl.pallas_call(
        _mixer_kernel,
        grid=(bsz, nj),
        in_specs=in_specs,
        out_specs=out_specs,
        out_shape=out_shape,
        scratch_shapes=[
            pltpu.VMEM((N_HEADS, HEAD_DIM, HEAD_DIM), F32),
            pltpu.VMEM((N_EXPERTS, TM), F32),
            pltpu.VMEM((TM, d), BF16),
        ],
        compiler_params=pltpu.CompilerParams(
            dimension_semantics=("arbitrary", "arbitrary"), vmem_limit_bytes=VMEM_LIMIT),
        name="mixer_router",
    )(x, shm, scm, gtm, shf, scf, g_mix, g_ffn, w_in_b, ln_g, ln_b, w_sp, bs_full, rn_g,
      w_out_b, cos_t, sin_t, dmat, qdec, kdec, cdec, wr_t, br)


def _run_copies(tab_ref, make_copy):
    def per_expert(e, carry):
        slot0 = tab_ref[e]
        local0 = tab_ref[2 * N_EXPERTS + e]

        def piece(p, c2):
            off = p * GROUP
            make_copy(pl.multiple_of(local0 + off, GROUP), pl.multiple_of(slot0 + off, GROUP)).start()
            return c2

        lax.fori_loop(0, tab_ref[N_EXPERTS + e], piece, 0)
        return carry

    lax.fori_loop(0, N_EXPERTS, per_expert, 0)


def _wait_copies(n_pieces, one_copy):
    def body(p, c):
        one_copy.wait()
        return c

    lax.fori_loop(0, n_pieces, body, 0)


def _dispatch_kernel(cntp_ref, pstart_ref, padded_ref, tab_ref, lpos_ref, h2_ref, xs_hbm,
                     xl_ref, zblk, pending_ref, sems, zsem):
    i = pl.program_id(0)
    par = i % 2
    lp = lpos_ref[...]
    r_io = lax.broadcasted_iota(I32, (LOCAL_ROWS, TM), 0)
    hit = r_io == lp[0:1]
    for kk in range(1, TOP_K):
        hit = hit | (r_io == lp[kk:kk + 1])
    perm = jnp.where(hit, 1.0, 0.0).astype(BF16)
    xl_ref[par] = jnp.dot(perm, h2_ref[...], preferred_element_type=F32)

    def make_copy_into(slot):
        def make_copy(local_row, slot_row):
            return pltpu.make_async_copy(xl_ref.at[slot, pl.ds(local_row, GROUP)],
                                         xs_hbm.at[pl.ds(slot_row, GROUP)], sems.at[slot])
        return make_copy

    _run_copies(tab_ref, make_copy_into(par))

    @pl.when(i > 0)
    def _():
        _wait_copies(pending_ref[0], make_copy_into(1 - par)(0, 0))

    pending_ref[0] = tab_ref[3 * N_EXPERTS]

    @pl.when(i == 0)
    def _():
        zblk[...] = jnp.zeros_like(zblk)

        def per_expert(e, carry):
            start = pstart_ref[e]

            def zcopy(r):
                return pltpu.make_async_copy(
                    zblk.at[pl.ds(0, GROUP)],
                    xs_hbm.at[pl.ds(pl.multiple_of(start + r * GROUP, GROUP), GROUP)], zsem)

            def fill(r, c2):
                zcopy(r).start()
                return c2

            def drain(r, c2):
                zcopy(r).wait()
                return c2

            lax.fori_loop(cntp_ref[e] // GROUP, padded_ref[e] // GROUP, fill, 0)
            lax.fori_loop(cntp_ref[e] // GROUP, padded_ref[e] // GROUP, drain, 0)
            return carry

        lax.fori_loop(0, N_EXPERTS, per_expert, 0)

        first_tail = (pstart_ref[N_EXPERTS - 1] + padded_ref[N_EXPERTS - 1]) // BM

        def tail(blk, carry):
            cp = pltpu.make_async_copy(
                zblk, xs_hbm.at[pl.ds(pl.multiple_of(blk * BM, BM), BM)], zsem)
            cp.start()
            cp.wait()
            return carry

        lax.fori_loop(first_tail, xs_hbm.shape[0] // BM, tail, 0)

    @pl.when(i == pl.num_programs(0) - 1)
    def _():
        _wait_copies(pending_ref[0], make_copy_into(par)(0, 0))


def _dispatch_call(cntp, pstart, padded, tab, lpos8, h2, n_slots):
    n_tok, d = h2.shape
    grid_spec = pltpu.PrefetchScalarGridSpec(
        num_scalar_prefetch=3,
        grid=(n_tok // TM,),
        in_specs=[
            pl.BlockSpec((TAB,), lambda i, *_: (i,), memory_space=pltpu.SMEM),
            pl.BlockSpec((8, TM), lambda i, *_: (0, i)),
            pl.BlockSpec((TM, d), lambda i, *_: (i, 0)),
        ],
        out_specs=pl.BlockSpec(memory_space=pl.ANY),
        scratch_shapes=[pltpu.VMEM((2, LOCAL_ROWS, d), F32), pltpu.VMEM((BM, d), F32),
                        pltpu.SMEM((1,), I32), pltpu.SemaphoreType.DMA((2,)),
                        pltpu.SemaphoreType.DMA],
    )
    return pl.pallas_call(
        _dispatch_kernel,
        grid_spec=grid_spec,
        out_shape=jax.ShapeDtypeStruct((n_slots, d), F32),
        compiler_params=pltpu.CompilerParams(
            dimension_semantics=("arbitrary",), vmem_limit_bytes=VMEM_LIMIT),
        name="moe_dispatch",
    )(cntp, pstart, padded, tab, lpos8, h2)


def _ffn_kernel(be_ref, bsrc_ref, bvalid_ref, xs_ref, w1_ref, b1_ref, w2_ref, b2_ref, y_ref):
    i = pl.program_id(0)

    @pl.when(bvalid_ref[i] == 1)
    def _():
        a = jnp.dot(xs_ref[...].astype(BF16), w1_ref[0], preferred_element_type=F32) + b1_ref[0]
        a_glu = jnp.minimum(a[:, :EXPERT_FF], SWIGLU_LIMIT)
        a_lin = jnp.clip(a[:, EXPERT_FF:], -SWIGLU_LIMIT, SWIGLU_LIMIT)
        act = a_glu * jax.nn.sigmoid(SWIGLU_ALPHA * a_glu) * (a_lin + 1.0)
        y_ref[...] = jnp.dot(act.astype(BF16), w2_ref[0], preferred_element_type=F32) + b2_ref[0]

    @pl.when(bvalid_ref[i] == 0)
    def _():
        y_ref[...] = jnp.zeros_like(y_ref)


def _ffn_call(block_expert, block_src, block_valid, xs, w1_b, b1, w2_b, b2):
    n_slots, d = xs.shape
    n_blocks = n_slots // BM
    grid_spec = pltpu.PrefetchScalarGridSpec(
        num_scalar_prefetch=3,
        grid=(n_blocks,),
        in_specs=[
            pl.BlockSpec((BM, d), lambda i, be, bs, bv: (bs[i], 0)),
            pl.BlockSpec((1, d, 2 * EXPERT_FF), lambda i, be, bs, bv: (be[i], 0, 0)),
            pl.BlockSpec((1, 1, 2 * EXPERT_FF), lambda i, be, bs, bv: (be[i], 0, 0)),
            pl.BlockSpec((1, EXPERT_FF, d), lambda i, be, bs, bv: (be[i], 0, 0)),
            pl.BlockSpec((1, 1, d), lambda i, be, bs, bv: (be[i], 0, 0)),
        ],
        out_specs=pl.BlockSpec((BM, d), lambda i, be, bs, bv: (i, 0)),
    )
    return pl.pallas_call(
        _ffn_kernel,
        grid_spec=grid_spec,
        out_shape=jax.ShapeDtypeStruct((n_slots, d), F32),
        compiler_params=pltpu.CompilerParams(
            dimension_semantics=("arbitrary",), vmem_limit_bytes=VMEM_LIMIT),
        name="moe_ffn",
    )(block_expert, block_src, block_valid, xs, w1_b, b1, w2_b, b2)


def _combine_kernel(tab_ref, tabn_ref, lpos_ref, w_ref, x1_ref, gt_ref, gfin_ref, y_hbm, o_ref,
                    yl_ref, sems, *, final):
    i = pl.program_id(0)
    par = i % 2

    def make_copy_into(slot):
        def make_copy(local_row, slot_row):
            return pltpu.make_async_copy(y_hbm.at[pl.ds(slot_row, GROUP)],
                                         yl_ref.at[slot, pl.ds(local_row, GROUP)], sems.at[slot])
        return make_copy

    @pl.when(i == 0)
    def _():
        yl_ref[...] = jnp.zeros_like(yl_ref)
        _run_copies(tab_ref, make_copy_into(0))

    @pl.when(i + 1 < pl.num_programs(0))
    def _():
        _run_copies(tabn_ref, make_copy_into(1 - par))

    lp = lpos_ref[...]
    w = w_ref[...]
    l_io = lax.broadcasted_iota(I32, (TM, LOCAL_ROWS), 1)
    unperm = jnp.where(l_io == lp[:, 0:1], w[:, 0:1], 0.0)
    for kk in range(1, TOP_K):
        unperm = unperm + jnp.where(l_io == lp[:, kk:kk + 1], w[:, kk:kk + 1], 0.0)
    _wait_copies(tab_ref[3 * N_EXPERTS], make_copy_into(par)(0, 0))
    acc = jnp.dot(unperm.astype(BF16), yl_ref[par].astype(BF16), preferred_element_type=F32)
    out = x1_ref[...] + gt_ref[0] * acc
    if final:
        out = out * lax.rsqrt(jnp.mean(out * out, axis=-1, keepdims=True) + EPS) * gfin_ref[...]
    o_ref[...] = out


def _combine_call(tab, lpos_tok, w_tok, x1, gtf, g_final, y, seq, final):
    n_tok, d = x1.shape
    per_b = seq // TM
    n_tiles = n_tok // TM
    return pl.pallas_call(
        functools.partial(_combine_kernel, final=final),
        grid=(n_tiles,),
        in_specs=[
            pl.BlockSpec((TAB,), lambda i: (i,), memory_space=pltpu.SMEM),
            pl.BlockSpec((TAB,), lambda i: (jnp.minimum(i + 1, n_tiles - 1),),
                         memory_space=pltpu.SMEM),
            pl.BlockSpec((TM, TOP_K), lambda i: (i, 0)),
            pl.BlockSpec((TM, TOP_K), lambda i: (i, 0)),
            pl.BlockSpec((TM, d), lambda i: (i, 0)),
            pl.BlockSpec((1, 1, d), lambda i: (i // per_b, 0, 0)),
            pl.BlockSpec((1, d), lambda i: (0, 0)),
            pl.BlockSpec(memory_space=pl.ANY),
        ],
        out_specs=pl.BlockSpec((TM, d), lambda i: (i, 0)),
        out_shape=jax.ShapeDtypeStruct((n_tok, d), F32),
        scratch_shapes=[pltpu.VMEM((2, LOCAL_ROWS, d), F32), pltpu.SemaphoreType.DMA((2,))],
        compiler_params=pltpu.CompilerParams(
            dimension_semantics=("arbitrary",), vmem_limit_bytes=VMEM_LIMIT),
        name="moe_combine",
    )(tab, tab, lpos_tok, w_tok, x1, gtf, g_final, y)


def _tables(seq):
    half = HEAD_DIM // 2
    inv_freq = ROPE_BASE ** (-jnp.arange(half, dtype=F32) / half)
    ang = jnp.arange(seq, dtype=F32)[:, None] * inv_freq[None, :]
    cos = jnp.cos(ang)
    sin = jnp.sin(ang)
    cos_t = jnp.concatenate([cos, cos], axis=-1)
    sin_t = jnp.concatenate([-sin, sin], axis=-1)
    gamma = 1.0 - jnp.exp2(-5.0 - jnp.arange(N_HEADS, dtype=F32))
    log_g = jnp.log(gamma)
    idx = jnp.arange(CHUNK, dtype=F32)
    diff = idx[:, None] - idx[None, :]
    dmat = jnp.where(diff[None] >= 0,
                     jnp.exp(log_g[:, None, None] * jnp.maximum(diff, 0.0)[None]), 0.0)
    k_decay = jnp.exp(log_g[:, None] * (CHUNK - 1.0 - idx)[None, :])
    q_decay = jnp.exp(log_g[:, None] * (idx + 1.0)[None, :])
    chunk_decay = jnp.exp(log_g * CHUNK)
    widen = lambda t: jnp.repeat(t.T[:, :, None], HEAD_DIM, axis=2).reshape(CHUNK, GROUP_W)
    qdec = widen(q_decay)
    kdec = widen(k_decay)
    cdec = jnp.repeat(chunk_decay[:, None], HEAD_DIM, axis=1).reshape(1, GROUP_W)
    return cos_t, sin_t, dmat, qdec, kdec, cdec


def kernel(x, c, w_ada, b_ada, g_mix, w_in, gmlp_ln_g, gmlp_ln_b, w_spatial, b_spatial,
           ret_norm_g, w_out, g_ffn, w_router, b_router, w1, b1, w2, b2, g_final):
    bsz, seq, d = x.shape
    depth = w_in.shape[0]
    n_tok = bsz * seq
    n_tiles = n_tok // TM
    n_slots = n_tok * TOP_K + n_tiles * N_EXPERTS * GROUP + N_EXPERTS * BM
    assert n_slots % BM == 0
    n_blocks = n_slots // BM

    mod = _ada_call(c, w_ada, b_ada)
    cos_t, sin_t, dmat, qdec, kdec, cdec = _tables(seq)

    w_in_b = w_in.astype(BF16)
    w_out_b = w_out.astype(BF16)
    wr_t = jnp.swapaxes(w_router, 1, 2).astype(BF16)
    br = b_router.reshape(depth, N_EXPERTS, 1)
    w1_b = w1.astype(BF16).reshape(depth * N_EXPERTS, d, 2 * EXPERT_FF)
    w2_b = w2.astype(BF16).reshape(depth * N_EXPERTS, EXPERT_FF, d)
    b1_r = b1.reshape(depth * N_EXPERTS, 1, 2 * EXPERT_FF)
    b2_r = b2.reshape(depth * N_EXPERTS, 1, d)
    bs_full = jnp.repeat(jnp.swapaxes(b_spatial, 1, 2)[:, :, :, None], HEAD_DIM, axis=3)
    bs_full = bs_full.reshape(depth, CHUNK, GROUP_W)
    row3 = lambda a: a.reshape(depth, 1, a.shape[-1])
    gfin = g_final.reshape(1, d)
    e_ids = jnp.arange(N_EXPERTS, dtype=I32)
    blk = jnp.arange(n_blocks, dtype=I32)

    for l in range(depth):
        mods = [m.reshape(bsz, 1, d) for m in jnp.split(mod[l], N_MOD, axis=-1)]
        shm, scm, gtm, shf, scf, gtf = mods
        x1, h2, lpos8, wgt8, tab3, cnt = _mixer_call(
            l, x, (shm, scm, gtm, shf, scf), row3(g_mix), row3(g_ffn), w_in_b,
            row3(gmlp_ln_g), row3(gmlp_ln_b), w_spatial, bs_full, row3(ret_norm_g), w_out_b,
            cos_t, sin_t, dmat, qdec, kdec, cdec, wr_t, br)

        cntp = cnt[:, 0].astype(I32)
        padded = ((cntp + BM - 1) // BM) * BM
        pend = jnp.sum(jnp.where(e_ids[None, :] <= e_ids[:, None], padded[None, :], 0), axis=1)
        pstart = pend - padded
        gbase, runs, lstart = tab3[:, 0, :, 0], tab3[:, 1, :, 0], tab3[:, 2, :, 0]
        pieces = runs // GROUP
        tab = jnp.concatenate(
            [pstart[None, :] + gbase, pieces, lstart, jnp.sum(pieces, axis=1, keepdims=True),
             jnp.zeros((n_tiles, TAB - 3 * N_EXPERTS - 1), I32)], axis=1).reshape(-1)
        n_used = pend[-1] // BM
        block_valid = (blk < n_used).astype(I32)
        block_expert = jnp.minimum(
            jnp.sum((pend[None, :] <= (blk * BM)[:, None]).astype(I32), axis=1),
            N_EXPERTS - 1) + l * N_EXPERTS
        block_src = jnp.minimum(blk, n_used - 1).astype(I32)

        xs = _dispatch_call(cntp, pstart, padded, tab, lpos8, h2, n_slots)
        y = _ffn_call(block_expert, block_src, block_valid, xs, w1_b, b1_r, w2_b, b2_r)
        x = _combine_call(tab, lpos8[:TOP_K].T, wgt8[:TOP_K].T, x1.reshape(n_tok, d), gtf, gfin,
                          y, seq, final=(l == depth - 1)).reshape(bsz, seq, d)
    return x
```

```python
import functools

import jax
import jax.numpy as jnp
from jax import lax
from jax.experimental import pallas as pl
from jax.experimental.pallas import tpu as pltpu

F32 = jnp.float32
BF16 = jnp.bfloat16
I32 = jnp.int32

D_MODEL = 1024
CHUNK = 128
N_HEADS = 4
HEAD_DIM = 128
GROUP_W = N_HEADS * HEAD_DIM
N_EXPERTS = 32
TOP_K = 4
EXPERT_FF = 1024
N_MOD = 6
EPS = 1e-5
ROPE_BASE = 10000.0
SWIGLU_ALPHA = 1.702
SWIGLU_LIMIT = 7.0

TM = 256
BM = 512
GROUP = 8
PAD_ROWS = N_EXPERTS * GROUP
LOCAL_ROWS = TOP_K * TM + PAD_ROWS
PIECES = BM // GROUP
TAB = 1024
VMEM_LIMIT = 56 * 1024 * 1024


def _gelu_tanh(x):
    return x * (0.5 * (1.0 + jnp.tanh(0.7978845608028654 * (x + 0.044715 * (x * x * x)))))


def _silu(x):
    return x * jax.nn.sigmoid(x)


def _ada_kernel(c_ref, w_ref, b_ref, o_ref):
    ca = _silu(c_ref[...]).astype(BF16)
    o_ref[0] = jnp.dot(ca, w_ref[0].astype(BF16), preferred_element_type=F32) + b_ref[0]


def _ada_call(c, w_ada, b_ada):
    depth, d, _ = w_ada.shape
    bsz = c.shape[0]
    return pl.pallas_call(
        _ada_kernel,
        grid=(depth, N_MOD),
        in_specs=[
            pl.BlockSpec((bsz, d), lambda l, n: (0, 0)),
            pl.BlockSpec((1, d, d), lambda l, n: (l, 0, n)),
            pl.BlockSpec((1, 1, d), lambda l, n: (l, 0, n)),
        ],
        out_specs=pl.BlockSpec((1, bsz, d), lambda l, n: (l, 0, n)),
        out_shape=jax.ShapeDtypeStruct((depth, bsz, N_MOD * d), F32),
        name="adaln_mod",
    )(c, w_ada, b_ada.reshape(depth, 1, N_MOD * d))


def _unpermute_weighted(lpos_tok, w_tok, yl):
    l_io = lax.broadcasted_iota(I32, (TM, LOCAL_ROWS), 1)
    unperm = jnp.where(l_io == lpos_tok[:, 0:1], w_tok[:, 0:1], 0.0)
    for kk in range(1, TOP_K):
        unperm = unperm + jnp.where(l_io == lpos_tok[:, kk:kk + 1], w_tok[:, kk:kk + 1], 0.0)
    return jnp.dot(unperm.astype(BF16), yl.astype(BF16), preferred_element_type=F32)


def _mixer_kernel(*refs, fuse_prev):
    if fuse_prev:
        (ylp_ref, lposp_ref, wp_ref, gtp_ref), refs = refs[:4], refs[4:]
    (x_ref, shm_ref, scm_ref, gtm_ref, shf_ref, scf_ref, gmix_ref, gffn_ref,
     win_ref, lng_ref, lnb_ref, ws_ref, bs_ref, rng_ref, wout_ref,
     cos_ref, sin_ref, dmat_ref, qdec_ref, kdec_ref, cdec_ref, wr_ref, br_ref,
     x1_ref, xl_ref, lpos_ref, wgt_ref, tab_ref, cnt_ref,
     state_ref, base_ref, ob_ref) = refs
    b = pl.program_id(0)
    j = pl.program_id(1)
    n_chunks = TM // CHUNK

    @pl.when(j == 0)
    def _():
        state_ref[...] = jnp.zeros_like(state_ref)

    @pl.when((b == 0) & (j == 0))
    def _():
        base_ref[...] = jnp.zeros_like(base_ref)

    x = x_ref[0]
    if fuse_prev:
        x = x + gtp_ref[0] * _unpermute_weighted(lposp_ref[...], wp_ref[...], ylp_ref[0])

    ms = jnp.mean(x * x, axis=-1, keepdims=True)
    h = x * lax.rsqrt(ms + EPS) * (gmix_ref[0] * (1.0 + scm_ref[0])) + shm_ref[0]
    hb = h.astype(BF16)

    def proj(k):
        return jnp.dot(hb, win_ref[0, :, k * GROUP_W:(k + 1) * GROUP_W],
                       preferred_element_type=F32)

    def rows(c):
        return slice(c * CHUNK, (c + 1) * CHUNK)

    def lanes(hh):
        return slice(hh * HEAD_DIM, (hh + 1) * HEAD_DIM)

    gu = _gelu_tanh(proj(0))
    gv = _gelu_tanh(proj(1))
    mu = jnp.mean(gv, axis=-1, keepdims=True)
    vc = gv - mu
    var = jnp.mean(vc * vc, axis=-1, keepdims=True)
    vn = (vc * lax.rsqrt(var + EPS) * lng_ref[0] + lnb_ref[0]).astype(BF16)
    t_io = lax.broadcasted_iota(I32, (CHUNK, CHUNK), 0)
    s_io = lax.broadcasted_iota(I32, (CHUNK, CHUNK), 1)
    causal = t_io >= s_io
    bias = bs_ref[0]
    for hh in range(N_HEADS):
        wm = jnp.where(causal, ws_ref[0, hh], 0.0).astype(BF16)
        for c in range(n_chunks):
            mixed = jnp.dot(wm, vn[rows(c), lanes(hh)], preferred_element_type=F32)
            oa = gu[rows(c), lanes(hh)] * (mixed + bias[:, lanes(hh)])
            ob_ref[rows(c), lanes(hh)] = oa.astype(BF16)

    q = proj(2)
    k = proj(3)
    vr = proj(4).astype(BF16)
    g = proj(5)
    cos = cos_ref[...]
    sin = sin_ref[...]
    k_scale = HEAD_DIM ** -0.5
    for hh in range(N_HEADS):
        qh = q[:, lanes(hh)]
        kh = k[:, lanes(hh)]
        qh = qh * cos + pltpu.roll(qh, HEAD_DIM // 2, 1) * sin
        kh = (kh * cos + pltpu.roll(kh, HEAD_DIM // 2, 1) * sin) * k_scale
        dm = dmat_ref[hh]
        qd = qdec_ref[:, lanes(hh)]
        kd = kdec_ref[:, lanes(hh)]
        cd = cdec_ref[:, lanes(hh)]
        gn = rng_ref[0][:, lanes(hh)]
        for c in range(n_chunks):
            qc = qh[rows(c)]
            kc = kh[rows(c)]
            vcb = vr[rows(c), lanes(hh)]
            s = lax.dot_general(qc.astype(BF16), kc.astype(BF16),
                                (((1,), (1,)), ((), ())), preferred_element_type=F32) * dm
            st = state_ref[hh]
            o = jnp.dot(s.astype(BF16), vcb, preferred_element_type=F32)
            o = o + jnp.dot((qc * qd).astype(BF16), st.astype(BF16), preferred_element_type=F32)
            kdt = (kc * kd).T.astype(BF16)
            state_ref[hh] = st * cd + jnp.dot(kdt, vcb, preferred_element_type=F32)
            o = o * lax.rsqrt(jnp.mean(o * o, axis=-1, keepdims=True) + EPS) * gn
            ob = o * _silu(g[rows(c), lanes(hh)])
            ob_ref[rows(c), GROUP_W + hh * HEAD_DIM:GROUP_W + (hh + 1) * HEAD_DIM] = ob.astype(BF16)

    y = jnp.dot(ob_ref[...], wout_ref[0], preferred_element_type=F32)
    x1 = x + gtm_ref[0] * y
    x1_ref[0] = x1

    ms2 = jnp.mean(x1 * x1, axis=-1, keepdims=True)
    h2 = x1 * lax.rsqrt(ms2 + EPS) * (gffn_ref[0] * (1.0 + scf_ref[0])) + shf_ref[0]
    h2b = h2.astype(BF16)

    lg = lax.dot_general(wr_ref[0], h2b, (((1,), (1,)), ((), ())),
                         preferred_element_type=F32) + br_ref[0]
    e_io = lax.broadcasted_iota(I32, (N_EXPERTS, TM), 0)
    work = lg
    vals, ids = [], []
    for _ in range(TOP_K):
        m = jnp.max(work, axis=0, keepdims=True)
        ik = jnp.min(jnp.where(work == m, e_io, N_EXPERTS), axis=0, keepdims=True)
        vals.append(m)
        ids.append(ik)
        work = jnp.where(e_io == ik, -jnp.inf, work)
    exps = [jnp.exp(v - vals[0]) for v in vals]
    denom = exps[0] + exps[1] + exps[2] + exps[3]
    wts = [e / denom for e in exps]

    sel = [e_io == ik for ik in ids]
    oh = jnp.where(sel[0] | sel[1] | sel[2] | sel[3], 1.0, 0.0)
    r_io = lax.broadcasted_iota(I32, (TM, TM), 0)
    c_io = lax.broadcasted_iota(I32, (TM, TM), 1)
    upper = jnp.where(r_io < c_io, 1.0, 0.0).astype(BF16)
    cum = jnp.dot(oh.astype(BF16), upper, preferred_element_type=F32)
    cnt_t = jnp.sum(oh, axis=1, keepdims=True)
    run = jnp.ceil(cnt_t * (1.0 / GROUP)) * GROUP
    run_b = jnp.broadcast_to(run, (N_EXPERTS, TM))
    er_io = lax.broadcasted_iota(I32, (N_EXPERTS, N_EXPERTS), 0)
    ec_io = lax.broadcasted_iota(I32, (N_EXPERTS, N_EXPERTS), 1)
    lower = jnp.where(ec_io < er_io, 1.0, 0.0).astype(BF16)
    lstart = jnp.dot(lower, run_b.astype(BF16), preferred_element_type=F32)
    pos = lstart + cum
    lpos = [jnp.sum(jnp.where(sk, pos, 0.0), axis=0, keepdims=True) for sk in sel]
    gbase = base_ref[...]
    base_ref[...] = gbase + run_b
    cnt_ref[...] = base_ref[:, :128]
    tab_ref[0, 0] = gbase[:, :128].astype(I32)
    tab_ref[0, 1] = run_b[:, :128].astype(I32)
    tab_ref[0, 2] = lstart[:, :128].astype(I32)

    row8 = lax.broadcasted_iota(I32, (8, TM), 0)

    def stack4(parts, fill):
        out = jnp.full((8, TM), fill, parts[0].dtype)
        for kk in range(TOP_K):
            out = jnp.where(row8 == kk, parts[kk], out)
        return out

    lpos_i = [p.astype(I32) for p in lpos]
    lpos_ref[...] = stack4(lpos_i, 0)
    wgt_ref[...] = stack4(wts, 0.0)

    lr_io = lax.broadcasted_iota(I32, (LOCAL_ROWS, TM), 0)
    hit = lr_io == lpos_i[0]
    for kk in range(1, TOP_K):
        hit = hit | (lr_io == lpos_i[kk])
    perm = jnp.where(hit, 1.0, 0.0).astype(BF16)
    xl_ref[0] = jnp.dot(perm, h2b, preferred_element_type=F32)


def _mixer_call(layer, x, mods, g_mix, g_ffn, w_in_b, ln_g, ln_b, w_sp, bs_full, rn_g,
                w_out_b, cos_t, sin_t, dmat, qdec, kdec, cdec, wr_t, br, prev=None):
    bsz, seq, d = x.shape
    n_tok = bsz * seq
    nj = seq // TM
    n_tiles = bsz * nj
    shm, scm, gtm, shf, scf = mods
    vec = lambda: pl.BlockSpec((1, 1, d), lambda b, j: (b, 0, 0))
    lvec = lambda w: pl.BlockSpec((1, 1, w), lambda b, j: (layer, 0, 0))
    full = lambda shp: pl.BlockSpec(shp, lambda b, j: (0,) * len(shp))
    tok = lambda shp: pl.BlockSpec(shp, lambda b, j: (0, b * nj + j))
    in_specs = [
        pl.BlockSpec((1, TM, d), lambda b, j: (b, j, 0)),
        vec(), vec(), vec(), vec(), vec(),
        lvec(d), lvec(d),
        pl.BlockSpec((1, d, 6 * GROUP_W), lambda b, j: (layer, 0, 0)),
        lvec(GROUP_W), lvec(GROUP_W),
        pl.BlockSpec((1, N_HEADS, CHUNK, CHUNK), lambda b, j: (layer, 0, 0, 0)),
        pl.BlockSpec((1, CHUNK, GROUP_W), lambda b, j: (layer, 0, 0)),
        lvec(GROUP_W),
        pl.BlockSpec((1, d, d), lambda b, j: (layer, 0, 0)),
        pl.BlockSpec((TM, HEAD_DIM), lambda b, j: (j, 0)),
        pl.BlockSpec((TM, HEAD_DIM), lambda b, j: (j, 0)),
        full((N_HEADS, CHUNK, CHUNK)),
        full((CHUNK, GROUP_W)), full((CHUNK, GROUP_W)), full((1, GROUP_W)),
        pl.BlockSpec((1, N_EXPERTS, d), lambda b, j: (layer, 0, 0)),
        pl.BlockSpec((1, N_EXPERTS, 1), lambda b, j: (layer, 0, 0)),
    ]
    out_specs = [
        pl.BlockSpec((1, TM, d), lambda b, j: (b, j, 0)),
        pl.BlockSpec((1, LOCAL_ROWS, d), lambda b, j: (b * nj + j, 0, 0)),
        tok((8, TM)), tok((8, TM)),
        pl.BlockSpec((1, 3, N_EXPERTS, 128), lambda b, j: (b * nj + j, 0, 0, 0)),
        pl.BlockSpec((N_EXPERTS, 128), lambda b, j: (0, 0)),
    ]
    out_shape = [
        jax.ShapeDtypeStruct((bsz, seq, d), F32),
        jax.ShapeDtypeStruct((n_tiles, LOCAL_ROWS, d), F32),
        jax.ShapeDtypeStruct((8, n_tok), I32),
        jax.ShapeDtypeStruct((8, n_tok), F32),
        jax.ShapeDtypeStruct((n_tiles, 3, N_EXPERTS, 128), I32),
        jax.ShapeDtypeStruct((N_EXPERTS, 128), F32),
    ]
    args = (x, shm, scm, gtm, shf, scf, g_mix, g_ffn, w_in_b, ln_g, ln_b, w_sp, bs_full, rn_g,
            w_out_b, cos_t, sin_t, dmat, qdec, kdec, cdec, wr_t, br)
    if prev is not None:
        in_specs = [
            pl.BlockSpec((1, LOCAL_ROWS, d), lambda b, j: (b * nj + j, 0, 0)),
            pl.BlockSpec((TM, TOP_K), lambda b, j: (b * nj + j, 0)),
            pl.BlockSpec((TM, TOP_K), lambda b, j: (b * nj + j, 0)),
            vec(),
        ] + in_specs
        args = tuple(prev) + args
    return pl.pallas_call(
        functools.partial(_mixer_kernel, fuse_prev=prev is not None),
        grid=(bsz, nj),
        in_specs=in_specs,
        out_specs=out_specs,
        out_shape=out_shape,
        scratch_shapes=[
            pltpu.VMEM((N_HEADS, HEAD_DIM, HEAD_DIM), F32),
            pltpu.VMEM((N_EXPERTS, TM), F32),
            pltpu.VMEM((TM, d), BF16),
        ],
        compiler_params=pltpu.CompilerParams(
            dimension_semantics=("arbitrary", "arbitrary"), vmem_limit_bytes=VMEM_LIMIT),
        name="mixer_router",
    )(*args)


def _ffn_kernel(be_ref, nused_ref, tab_ref, tabn_ref, w1_ref, b1_ref, w2_ref, b2_ref,
                xl_hbm, yl_hbm, xbuf, ybuf, zbuf, sem_in, sem_out, zsem):
    i = pl.program_id(0)
    par = i % 2
    n_used = nused_ref[0]
    n_tiles = yl_hbm.shape[0] // LOCAL_ROWS - 1

    def zero_copy(row0):
        return pltpu.make_async_copy(
            zbuf, yl_hbm.at[pl.ds(pl.multiple_of(row0, PAD_ROWS), PAD_ROWS)], zsem)

    n_zero = n_tiles + LOCAL_ROWS // PAD_ROWS

    def zero_row0(z):
        return jnp.where(z < n_tiles, z * LOCAL_ROWS + TOP_K * TM,
                         n_tiles * LOCAL_ROWS + (z - n_tiles) * PAD_ROWS)

    @pl.when(i == 0)
    def _():
        zbuf[...] = jnp.zeros_like(zbuf)

        def body(z, c):
            zero_copy(zero_row0(z)).start()
            return c
        lax.fori_loop(0, n_zero, body, 0)

    def start_fetch(tab, slot):
        def body(p, c):
            pltpu.make_async_copy(
                xl_hbm.at[pl.ds(pl.multiple_of(tab[p], GROUP), GROUP)],
                xbuf.at[slot, pl.ds(pl.multiple_of(p * GROUP, GROUP), GROUP)], sem_in.at[slot]).start()
            return c
        lax.fori_loop(0, PIECES, body, 0)

    def wait_fetch(slot):
        pltpu.make_async_copy(xl_hbm.at[pl.ds(0, BM)], xbuf.at[slot], sem_in.at[slot]).wait()

    def wait_store(slot):
        pltpu.make_async_copy(ybuf.at[slot], yl_hbm.at[pl.ds(0, BM)], sem_out.at[slot]).wait()

    @pl.when(i == 0)
    def _():
        start_fetch(tab_ref, 0)

    @pl.when(i + 1 < n_used)
    def _():
        start_fetch(tabn_ref, 1 - par)

    @pl.when(i < n_used)
    def _():
        wait_fetch(par)

        @pl.when(i >= 2)
        def _():
            wait_store(par)

        a = jnp.dot(xbuf[par].astype(BF16), w1_ref[0], preferred_element_type=F32) + b1_ref[0]
        a_glu = jnp.minimum(a[:, :EXPERT_FF], SWIGLU_LIMIT)
        a_lin = jnp.clip(a[:, EXPERT_FF:], -SWIGLU_LIMIT, SWIGLU_LIMIT)
        act = a_glu * jax.nn.sigmoid(SWIGLU_ALPHA * a_glu) * (a_lin + 1.0)
        ybuf[par] = jnp.dot(act.astype(BF16), w2_ref[0], preferred_element_type=F32) + b2_ref[0]

        @pl.when(i == 0)
        def _():
            def body(z, c):
                zero_copy(0).wait()
                return c
            lax.fori_loop(0, n_zero, body, 0)

        def store(p, c):
            pltpu.make_async_copy(
                ybuf.at[par, pl.ds(pl.multiple_of(p * GROUP, GROUP), GROUP)],
                yl_hbm.at[pl.ds(pl.multiple_of(tab_ref[PIECES + p], GROUP), GROUP)],
                sem_out.at[par]).start()
            return c
        lax.fori_loop(0, PIECES, store, 0)

        @pl.when(i == n_used - 1)
        def _():
            wait_store(par)

            @pl.when(i >= 1)
            def _():
                wait_store(1 - par)


def _ffn_call(block_expert, n_used, piece_tab, xl, w1_b, b1, w2_b, b2):
    n_tiles, _, d = xl.shape
    n_blocks = piece_tab.shape[0] // TAB
    xl2 = xl.reshape(-1, d)
    yl_shape = (n_tiles + 1, LOCAL_ROWS, d)
    grid_spec = pltpu.PrefetchScalarGridSpec(
        num_scalar_prefetch=2,
        grid=(n_blocks,),
        in_specs=[
            pl.BlockSpec((TAB,), lambda i, be, nu: (i,), memory_space=pltpu.SMEM),
            pl.BlockSpec((TAB,), lambda i, be, nu: (jnp.minimum(i + 1, n_blocks - 1),),
                         memory_space=pltpu.SMEM),
            pl.BlockSpec((1, d, 2 * EXPERT_FF), lambda i, be, nu: (be[i], 0, 0)),
            pl.BlockSpec((1, 1, 2 * EXPERT_FF), lambda i, be, nu: (be[i], 0, 0)),
            pl.BlockSpec((1, EXPERT_FF, d), lambda i, be, nu: (be[i], 0, 0)),
            pl.BlockSpec((1, 1, d), lambda i, be, nu: (be[i], 0, 0)),
            pl.BlockSpec(memory_space=pl.ANY),
        ],
        out_specs=pl.BlockSpec(memory_space=pl.ANY),
        scratch_shapes=[pltpu.VMEM((2, BM, d), F32), pltpu.VMEM((2, BM, d), F32),
                        pltpu.VMEM((PAD_ROWS, d), F32),
                        pltpu.SemaphoreType.DMA((2,)), pltpu.SemaphoreType.DMA((2,)),
                        pltpu.SemaphoreType.DMA],
    )
    out = pl.pallas_call(
        _ffn_kernel,
        grid_spec=grid_spec,
        out_shape=jax.ShapeDtypeStruct((yl_shape[0] * LOCAL_ROWS, d), F32),
        compiler_params=pltpu.CompilerParams(
            dimension_semantics=("arbitrary",), vmem_limit_bytes=VMEM_LIMIT),
        name="moe_ffn",
    )(block_expert, n_used, piece_tab, piece_tab, w1_b, b1, w2_b, b2, xl2)
    return out.reshape(yl_shape)


def _combine_kernel(yl_ref, lpos_ref, w_ref, x1_ref, gt_ref, gfin_ref, o_ref):
    out = x1_ref[...] + gt_ref[0] * _unpermute_weighted(lpos_ref[...], w_ref[...], yl_ref[0])
    o_ref[...] = out * lax.rsqrt(jnp.mean(out * out, axis=-1, keepdims=True) + EPS) * gfin_ref[...]


def _combine_call(yl, lpos_tok, w_tok, x1, gtf, g_final, seq):
    n_tok, d = x1.shape
    per_b = seq // TM
    return pl.pallas_call(
        _combine_kernel,
        grid=(n_tok // TM,),
        in_specs=[
            pl.BlockSpec((1, LOCAL_ROWS, d), lambda i: (i, 0, 0)),
            pl.BlockSpec((TM, TOP_K), lambda i: (i, 0)),
            pl.BlockSpec((TM, TOP_K), lambda i: (i, 0)),
            pl.BlockSpec((TM, d), lambda i: (i, 0)),
            pl.BlockSpec((1, 1, d), lambda i: (i // per_b, 0, 0)),
            pl.BlockSpec((1, d), lambda i: (0, 0)),
        ],
        out_specs=pl.BlockSpec((TM, d), lambda i: (i, 0)),
        out_shape=jax.ShapeDtypeStruct((n_tok, d), F32),
        compiler_params=pltpu.CompilerParams(
            dimension_semantics=("arbitrary",), vmem_limit_bytes=VMEM_LIMIT),
        name="moe_combine",
    )(yl, lpos_tok, w_tok, x1, gtf, g_final)


def _tables(seq):
    half = HEAD_DIM // 2
    inv_freq = ROPE_BASE ** (-jnp.arange(half, dtype=F32) / half)
    ang = jnp.arange(seq, dtype=F32)[:, None] * inv_freq[None, :]
    cos = jnp.cos(ang)
    sin = jnp.sin(ang)
    cos_t = jnp.concatenate([cos, cos], axis=-1)
    sin_t = jnp.concatenate([-sin, sin], axis=-1)
    gamma = 1.0 - jnp.exp2(-5.0 - jnp.arange(N_HEADS, dtype=F32))
    log_g = jnp.log(gamma)
    idx = jnp.arange(CHUNK, dtype=F32)
    diff = idx[:, None] - idx[None, :]
    dmat = jnp.where(diff[None] >= 0,
                     jnp.exp(log_g[:, None, None] * jnp.maximum(diff, 0.0)[None]), 0.0)
    k_decay = jnp.exp(log_g[:, None] * (CHUNK - 1.0 - idx)[None, :])
    q_decay = jnp.exp(log_g[:, None] * (idx + 1.0)[None, :])
    chunk_decay = jnp.exp(log_g * CHUNK)
    widen = lambda t: jnp.repeat(t.T[:, :, None], HEAD_DIM, axis=2).reshape(CHUNK, GROUP_W)
    qdec = widen(q_decay)
    kdec = widen(k_decay)
    cdec = jnp.repeat(chunk_decay[:, None], HEAD_DIM, axis=1).reshape(1, GROUP_W)
    return cos_t, sin_t, dmat, qdec, kdec, cdec


def kernel(x, c, w_ada, b_ada, g_mix, w_in, gmlp_ln_g, gmlp_ln_b, w_spatial, b_spatial,
           ret_norm_g, w_out, g_ffn, w_router, b_router, w1, b1, w2, b2, g_final):
    bsz, seq, d = x.shape
    depth = w_in.shape[0]
    n_tok = bsz * seq
    n_tiles = n_tok // TM
    n_slots = n_tok * TOP_K + n_tiles * N_EXPERTS * GROUP + N_EXPERTS * BM
    assert n_slots % BM == 0
    n_blocks = n_slots // BM

    mod = _ada_call(c, w_ada, b_ada)
    cos_t, sin_t, dmat, qdec, kdec, cdec = _tables(seq)

    w_in_b = w_in.astype(BF16)
    w_out_b = w_out.astype(BF16)
    wr_t = jnp.swapaxes(w_router, 1, 2).astype(BF16)
    br = b_router.reshape(depth, N_EXPERTS, 1)
    w1_b = w1.astype(BF16).reshape(depth * N_EXPERTS, d, 2 * EXPERT_FF)
    w2_b = w2.astype(BF16).reshape(depth * N_EXPERTS, EXPERT_FF, d)
    b1_r = b1.reshape(depth * N_EXPERTS, 1, 2 * EXPERT_FF)
    b2_r = b2.reshape(depth * N_EXPERTS, 1, d)
    bs_full = jnp.repeat(jnp.swapaxes(b_spatial, 1, 2)[:, :, :, None], HEAD_DIM, axis=3)
    bs_full = bs_full.reshape(depth, CHUNK, GROUP_W)
    row3 = lambda a: a.reshape(depth, 1, a.shape[-1])
    gfin = g_final.reshape(1, d)
    e_ids = jnp.arange(N_EXPERTS, dtype=I32)
    blk = jnp.arange(n_blocks, dtype=I32)

    piece_off = GROUP * jnp.arange(PIECES, dtype=I32)
    tile_row0 = LOCAL_ROWS * jnp.arange(n_tiles, dtype=I32)
    zero_row = LOCAL_ROWS - GROUP
    spare_row0 = n_tiles * LOCAL_ROWS
    prev = None
    for l in range(depth):
        mods = [m.reshape(bsz, 1, d) for m in jnp.split(mod[l], N_MOD, axis=-1)]
        shm, scm, gtm, shf, scf, gtf = mods
        x1, xl, lpos8, wgt8, tab3, cnt = _mixer_call(
            l, x, (shm, scm, gtm, shf, scf), row3(g_mix), row3(g_ffn), w_in_b,
            row3(gmlp_ln_g), row3(gmlp_ln_b), w_spatial, bs_full, row3(ret_norm_g), w_out_b,
            cos_t, sin_t, dmat, qdec, kdec, cdec, wr_t, br, prev=prev)

        cntp = cnt[:, 0].astype(I32)
        padded = ((cntp + BM - 1) // BM) * BM
        pend = jnp.sum(jnp.where(e_ids[None, :] <= e_ids[:, None], padded[None, :], 0), axis=1)
        pstart = pend - padded
        n_used = pend[-1] // BM
        block_e = jnp.minimum(
            jnp.sum((pend[None, :] <= (blk * BM)[:, None]).astype(I32), axis=1), N_EXPERTS - 1)
        gbase, runs, lstart = tab3[:, 0, :, 0], tab3[:, 1, :, 0], tab3[:, 2, :, 0]
        r = (blk * BM - pstart[block_e])[:, None] + piece_off[None, :]
        gb = gbase.T[block_e]
        rn = runs.T[block_e]
        delta = tile_row0[None, :] + lstart.T[block_e] - gb
        in_run = (gb[:, None, :] <= r[:, :, None]) & (r[:, :, None] < (gb + rn)[:, None, :])
        row = r + jnp.sum(jnp.where(in_run, delta[:, None, :], 0), axis=-1)
        live = (r < cntp[block_e][:, None]) & (blk < n_used)[:, None]
        src = jnp.where(live, row, zero_row)
        dst = jnp.where(live, row, spare_row0 + (blk % 2)[:, None] * BM + piece_off[None, :])
        piece_tab = jnp.concatenate(
            [src, dst, jnp.zeros((n_blocks, TAB - 2 * PIECES), I32)], axis=1).reshape(-1)

        yl = _ffn_call(block_e + l * N_EXPERTS, n_used.reshape(1), piece_tab, xl,
                       w1_b, b1_r, w2_b, b2_r)
        x = x1
        prev = (yl, lpos8[:TOP_K].T, wgt8[:TOP_K].T, gtf)
    out = _combine_call(prev[0], prev[1], prev[2], x.reshape(n_tok, d), prev[3], gfin, seq)
    return out.reshape(bsz, seq, d)
```

```python
import functools

import jax
import jax.numpy as jnp
from jax import lax
from jax.experimental import pallas as pl
from jax.experimental.pallas import tpu as pltpu

F32 = jnp.float32
BF16 = jnp.bfloat16
I32 = jnp.int32

D_MODEL = 1024
CHUNK = 128
N_HEADS = 4
HEAD_DIM = 128
GROUP_W = N_HEADS * HEAD_DIM
N_EXPERTS = 32
TOP_K = 4
EXPERT_FF = 1024
N_MOD = 6
EPS = 1e-5
ROPE_BASE = 10000.0
SWIGLU_ALPHA = 1.702
SWIGLU_LIMIT = 7.0

TM = 256
BM = 512
GROUP = 8
PAD_ROWS = N_EXPERTS * GROUP
LOCAL_ROWS = TOP_K * TM + PAD_ROWS
PIECES = BM // GROUP
COPY_UNROLL = 8
TAB = 1024
VMEM_LIMIT = 56 * 1024 * 1024


def _gelu_tanh(x):
    return x * (0.5 * (1.0 + jnp.tanh(0.7978845608028654 * (x + 0.044715 * (x * x * x)))))


def _silu(x):
    return x * jax.nn.sigmoid(x)


def _ada_kernel(c_ref, w_ref, b_ref, o_ref):
    ca = _silu(c_ref[...]).astype(BF16)
    o_ref[0] = jnp.dot(ca, w_ref[0].astype(BF16), preferred_element_type=F32) + b_ref[0]


def _ada_call(c, w_ada, b_ada):
    depth, d, _ = w_ada.shape
    bsz = c.shape[0]
    return pl.pallas_call(
        _ada_kernel,
        grid=(depth, N_MOD),
        in_specs=[
            pl.BlockSpec((bsz, d), lambda l, n: (0, 0)),
            pl.BlockSpec((1, d, d), lambda l, n: (l, 0, n)),
            pl.BlockSpec((1, 1, d), lambda l, n: (l, 0, n)),
        ],
        out_specs=pl.BlockSpec((1, bsz, d), lambda l, n: (l, 0, n)),
        out_shape=jax.ShapeDtypeStruct((depth, bsz, N_MOD * d), F32),
        name="adaln_mod",
    )(c, w_ada, b_ada.reshape(depth, 1, N_MOD * d))


def _unpermute_matrix(lpos_tok, w_tok):
    l_io = lax.broadcasted_iota(I32, (TM, LOCAL_ROWS), 1)
    unperm = jnp.where(l_io == lpos_tok[:, 0:1], w_tok[:, 0:1], 0.0)
    for kk in range(1, TOP_K):
        unperm = unperm + jnp.where(l_io == lpos_tok[:, kk:kk + 1], w_tok[:, kk:kk + 1], 0.0)
    return unperm.astype(BF16)


def _permute_matrix(lpos_rows):
    lr_io = lax.broadcasted_iota(I32, (LOCAL_ROWS, TM), 0)
    hit = lr_io == lpos_rows[0:1]
    for kk in range(1, TOP_K):
        hit = hit | (lr_io == lpos_rows[kk:kk + 1])
    return jnp.where(hit, 1.0, 0.0).astype(BF16)


def _mixer_kernel(*refs, fuse_prev, nj, n_tiles):
    if fuse_prev:
        (ylp_ref, lposp_ref, wp_ref, lposn_ref, wn_ref, gtp_ref), refs = refs[:6], refs[6:]
        unperm_sc, refs = refs[-1], refs[:-1]
    (x_ref, shm_ref, scm_ref, gtm_ref, shf_ref, scf_ref, gmix_ref, gffn_ref,
     win_ref, lng_ref, lnb_ref, ws_ref, bs_ref, rng_ref, wout_ref,
     cos_ref, sin_ref, dmat_ref, qdec_ref, kdec_ref, cdec_ref, wr_ref, br_ref,
     x1_ref, xl_ref, lpos_ref, wgt_ref, tab_ref, cnt_ref,
     state_ref, base_ref, ob_ref, h2_sc, lpos_sc) = refs
    t = pl.program_id(0)
    live = t < n_tiles
    j = jnp.minimum(t, n_tiles - 1) % nj
    n_chunks = TM // CHUNK

    @pl.when(j == 0)
    def _():
        state_ref[...] = jnp.zeros_like(state_ref)

    @pl.when(t == 0)
    def _():
        base_ref[...] = jnp.zeros_like(base_ref)
        h2_sc[...] = jnp.zeros_like(h2_sc)
        lpos_sc[...] = jnp.zeros_like(lpos_sc)
        if fuse_prev:
            unperm_sc[...] = _unpermute_matrix(lposp_ref[...], wp_ref[...])

    x = x_ref[0]
    if fuse_prev:
        x = x + gtp_ref[0] * jnp.dot(unperm_sc[...], ylp_ref[0].astype(BF16),
                                     preferred_element_type=F32)

    ms = jnp.mean(x * x, axis=-1, keepdims=True)
    h = x * lax.rsqrt(ms + EPS) * (gmix_ref[0] * (1.0 + scm_ref[0])) + shm_ref[0]
    hb = h.astype(BF16)
    if fuse_prev:
        unperm_sc[...] = _unpermute_matrix(lposn_ref[...], wn_ref[...])

    def proj(k):
        return jnp.dot(hb, win_ref[0, :, k * GROUP_W:(k + 1) * GROUP_W],
                       preferred_element_type=F32)

    def rows(c):
        return slice(c * CHUNK, (c + 1) * CHUNK)

    def lanes(hh):
        return slice(hh * HEAD_DIM, (hh + 1) * HEAD_DIM)

    gu = _gelu_tanh(proj(0))
    gv = _gelu_tanh(proj(1))
    mu = jnp.mean(gv, axis=-1, keepdims=True)
    vc = gv - mu
    var = jnp.mean(vc * vc, axis=-1, keepdims=True)
    vn = (vc * lax.rsqrt(var + EPS) * lng_ref[0] + lnb_ref[0]).astype(BF16)
    t_io = lax.broadcasted_iota(I32, (CHUNK, CHUNK), 0)
    s_io = lax.broadcasted_iota(I32, (CHUNK, CHUNK), 1)
    causal = t_io >= s_io
    bias = bs_ref[0]
    for hh in range(N_HEADS):
        wm = jnp.where(causal, ws_ref[0, hh], 0.0).astype(BF16)
        for c in range(n_chunks):
            mixed = jnp.dot(wm, vn[rows(c), lanes(hh)], preferred_element_type=F32)
            oa = gu[rows(c), lanes(hh)] * (mixed + bias[:, lanes(hh)])
            ob_ref[rows(c), lanes(hh)] = oa.astype(BF16)

    q = proj(2)
    k = proj(3)
    vr = proj(4).astype(BF16)
    g = proj(5)
    cos = cos_ref[...]
    sin = sin_ref[...]
    k_scale = HEAD_DIM ** -0.5
    for hh in range(N_HEADS):
        qh = q[:, lanes(hh)]
        kh = k[:, lanes(hh)]
        qh = qh * cos + pltpu.roll(qh, HEAD_DIM // 2, 1) * sin
        kh = (kh * cos + pltpu.roll(kh, HEAD_DIM // 2, 1) * sin) * k_scale
        dm = dmat_ref[hh]
        qd = qdec_ref[:, lanes(hh)]
        kd = kdec_ref[:, lanes(hh)]
        cd = cdec_ref[:, lanes(hh)]
        gn = rng_ref[0][:, lanes(hh)]
        for c in range(n_chunks):
            qc = qh[rows(c)]
            kc = kh[rows(c)]
            vcb = vr[rows(c), lanes(hh)]
            s = lax.dot_general(qc.astype(BF16), kc.astype(BF16),
                                (((1,), (1,)), ((), ())), preferred_element_type=F32) * dm
            st = state_ref[hh]
            o = jnp.dot(s.astype(BF16), vcb, preferred_element_type=F32)
            o = o + jnp.dot((qc * qd).astype(BF16), st.astype(BF16), preferred_element_type=F32)
            kdt = (kc * kd).T.astype(BF16)
            state_ref[hh] = st * cd + jnp.dot(kdt, vcb, preferred_element_type=F32)
            o = o * lax.rsqrt(jnp.mean(o * o, axis=-1, keepdims=True) + EPS) * gn
            ob = o * _silu(g[rows(c), lanes(hh)])
            ob_ref[rows(c), GROUP_W + hh * HEAD_DIM:GROUP_W + (hh + 1) * HEAD_DIM] = ob.astype(BF16)

    y = jnp.dot(ob_ref[...], wout_ref[0], preferred_element_type=F32)
    x1 = x + gtm_ref[0] * y

    ms2 = jnp.mean(x1 * x1, axis=-1, keepdims=True)
    h2 = x1 * lax.rsqrt(ms2 + EPS) * (gffn_ref[0] * (1.0 + scf_ref[0])) + shf_ref[0]
    h2b = h2.astype(BF16)

    xl_ref[0] = jnp.dot(_permute_matrix(lpos_sc[...]), h2_sc[...], preferred_element_type=F32)

    lg = lax.dot_general(wr_ref[0], h2b, (((1,), (1,)), ((), ())),
                         preferred_element_type=F32) + br_ref[0]
    e_io = lax.broadcasted_iota(I32, (N_EXPERTS, TM), 0)
    work = lg
    vals, ids = [], []
    for _ in range(TOP_K):
        m = jnp.max(work, axis=0, keepdims=True)
        ik = jnp.min(jnp.where(work == m, e_io, N_EXPERTS), axis=0, keepdims=True)
        vals.append(m)
        ids.append(ik)
        work = jnp.where(e_io == ik, -jnp.inf, work)
    exps = [jnp.exp(v - vals[0]) for v in vals]
    denom = exps[0] + exps[1] + exps[2] + exps[3]
    wts = [e / denom for e in exps]

    sel = [e_io == ik for ik in ids]
    oh = jnp.where(sel[0] | sel[1] | sel[2] | sel[3], 1.0, 0.0)
    r_io = lax.broadcasted_iota(I32, (TM, TM), 0)
    c_io = lax.broadcasted_iota(I32, (TM, TM), 1)
    upper = jnp.where(r_io < c_io, 1.0, 0.0).astype(BF16)
    cum = jnp.dot(oh.astype(BF16), upper, preferred_element_type=F32)
    cnt_t = jnp.sum(oh, axis=1, keepdims=True)
    run = jnp.ceil(cnt_t * (1.0 / GROUP)) * GROUP
    run_b = jnp.broadcast_to(run, (N_EXPERTS, TM))
    er_io = lax.broadcasted_iota(I32, (N_EXPERTS, N_EXPERTS), 0)
    ec_io = lax.broadcasted_iota(I32, (N_EXPERTS, N_EXPERTS), 1)
    lower = jnp.where(ec_io < er_io, 1.0, 0.0).astype(BF16)
    lstart = jnp.dot(lower, run_b.astype(BF16), preferred_element_type=F32)
    pos = lstart + cum
    lpos = [jnp.sum(jnp.where(sk, pos, 0.0), axis=0, keepdims=True) for sk in sel]
    gbase = base_ref[...]
    row8 = lax.broadcasted_iota(I32, (8, TM), 0)

    def stack4(parts, fill):
        out = jnp.full((8, TM), fill, parts[0].dtype)
        for kk in range(TOP_K):
            out = jnp.where(row8 == kk, parts[kk], out)
        return out

    lpos8 = stack4([p.astype(I32) for p in lpos], 0)
    wgt8 = stack4(wts, 0.0)

    @pl.when(live)
    def _():
        x1_ref[0] = x1
        base_ref[...] = gbase + run_b
        cnt_ref[...] = (gbase + run_b)[:, :128]
        tab_ref[0, 0] = gbase[:, :128].astype(I32)
        tab_ref[0, 1] = run_b[:, :128].astype(I32)
        tab_ref[0, 2] = lstart[:, :128].astype(I32)
        lpos_ref[...] = lpos8
        wgt_ref[...] = wgt8
        h2_sc[...] = h2b
        lpos_sc[...] = lpos8


def _mixer_call(layer, x, mods, g_mix, g_ffn, w_in_b, ln_g, ln_b, w_sp, bs_full, rn_g,
                w_out_b, cos_t, sin_t, dmat, qdec, kdec, cdec, wr_t, br, prev=None):
    bsz, seq, d = x.shape
    n_tok = bsz * seq
    nj = seq // TM
    n_tiles = bsz * nj
    shm, scm, gtm, shf, scf = mods
    tile = lambda t: jnp.minimum(t, n_tiles - 1)
    vec = lambda: pl.BlockSpec((1, 1, d), lambda t: (tile(t) // nj, 0, 0))
    lvec = lambda w: pl.BlockSpec((1, 1, w), lambda t: (layer, 0, 0))
    full = lambda shp: pl.BlockSpec(shp, lambda t: (0,) * len(shp))
    tok = lambda shp: pl.BlockSpec(shp, lambda t: (0, tile(t)))
    in_specs = [
        pl.BlockSpec((1, TM, d), lambda t: (tile(t) // nj, tile(t) % nj, 0)),
        vec(), vec(), vec(), vec(), vec(),
        lvec(d), lvec(d),
        pl.BlockSpec((1, d, 6 * GROUP_W), lambda t: (layer, 0, 0)),
        lvec(GROUP_W), lvec(GROUP_W),
        pl.BlockSpec((1, N_HEADS, CHUNK, CHUNK), lambda t: (layer, 0, 0, 0)),
        pl.BlockSpec((1, CHUNK, GROUP_W), lambda t: (layer, 0, 0)),
        lvec(GROUP_W),
        pl.BlockSpec((1, d, d), lambda t: (layer, 0, 0)),
        pl.BlockSpec((TM, HEAD_DIM), lambda t: (tile(t) % nj, 0)),
        pl.BlockSpec((TM, HEAD_DIM), lambda t: (tile(t) % nj, 0)),
        full((N_HEADS, CHUNK, CHUNK)),
        full((CHUNK, GROUP_W)), full((CHUNK, GROUP_W)), full((1, GROUP_W)),
        pl.BlockSpec((1, N_EXPERTS, d), lambda t: (layer, 0, 0)),
        pl.BlockSpec((1, N_EXPERTS, 1), lambda t: (layer, 0, 0)),
    ]
    out_specs = [
        pl.BlockSpec((1, TM, d), lambda t: (tile(t) // nj, tile(t) % nj, 0)),
        pl.BlockSpec((1, LOCAL_ROWS, d), lambda t: (jnp.maximum(t - 1, 0), 0, 0)),
        tok((8, TM)), tok((8, TM)),
        pl.BlockSpec((1, 3, N_EXPERTS, 128), lambda t: (tile(t), 0, 0, 0)),
        pl.BlockSpec((N_EXPERTS, 128), lambda t: (0, 0)),
    ]
    out_shape = [
        jax.ShapeDtypeStruct((bsz, seq, d), F32),
        jax.ShapeDtypeStruct((n_tiles, LOCAL_ROWS, d), F32),
        jax.ShapeDtypeStruct((8, n_tok), I32),
        jax.ShapeDtypeStruct((8, n_tok), F32),
        jax.ShapeDtypeStruct((n_tiles, 3, N_EXPERTS, 128), I32),
        jax.ShapeDtypeStruct((N_EXPERTS, 128), F32),
    ]
    args = (x, shm, scm, gtm, shf, scf, g_mix, g_ffn, w_in_b, ln_g, ln_b, w_sp, bs_full, rn_g,
            w_out_b, cos_t, sin_t, dmat, qdec, kdec, cdec, wr_t, br)
    scratch = [
        pltpu.VMEM((N_HEADS, HEAD_DIM, HEAD_DIM), F32),
        pltpu.VMEM((N_EXPERTS, TM), F32),
        pltpu.VMEM((TM, d), BF16),
        pltpu.VMEM((TM, d), BF16),
        pltpu.VMEM((8, TM), I32),
    ]
    if prev is not None:
        yl, lpos_tok, w_tok, gtf = prev
        nxt = lambda t: jnp.minimum(t + 1, n_tiles - 1)
        in_specs = [
            pl.BlockSpec((1, LOCAL_ROWS, d), lambda t: (tile(t), 0, 0)),
            pl.BlockSpec((TM, TOP_K), lambda t: (tile(t), 0)),
            pl.BlockSpec((TM, TOP_K), lambda t: (tile(t), 0)),
            pl.BlockSpec((TM, TOP_K), lambda t: (nxt(t), 0)),
            pl.BlockSpec((TM, TOP_K), lambda t: (nxt(t), 0)),
            vec(),
        ] + in_specs
        args = (yl, lpos_tok, w_tok, lpos_tok, w_tok, gtf) + args
        scratch.append(pltpu.VMEM((TM, LOCAL_ROWS), BF16))
    return pl.pallas_call(
        functools.partial(_mixer_kernel, fuse_prev=prev is not None, nj=nj, n_tiles=n_tiles),
        grid=(n_tiles + 1,),
        in_specs=in_specs,
        out_specs=out_specs,
        out_shape=out_shape,
        scratch_shapes=scratch,
        compiler_params=pltpu.CompilerParams(
            dimension_semantics=("arbitrary",), vmem_limit_bytes=VMEM_LIMIT),
        name="mixer_router",
    )(*args)


def _ffn_kernel(be_ref, nused_ref, tab_ref, tabn_ref, w1_ref, b1_ref, w2_ref, b2_ref,
                xl_hbm, yl_hbm, xbuf, ybuf, zbuf, sem_in, sem_out, zsem):
    i = pl.program_id(0)
    par = i % 2
    n_used = nused_ref[0]
    n_tiles = yl_hbm.shape[0] // LOCAL_ROWS - 1

    def zero_copy(row0):
        return pltpu.make_async_copy(
            zbuf, yl_hbm.at[pl.ds(pl.multiple_of(row0, PAD_ROWS), PAD_ROWS)], zsem)

    n_zero = n_tiles + LOCAL_ROWS // PAD_ROWS

    def zero_row0(z):
        return jnp.where(z < n_tiles, z * LOCAL_ROWS + TOP_K * TM,
                         n_tiles * LOCAL_ROWS + (z - n_tiles) * PAD_ROWS)

    @pl.when(i == 0)
    def _():
        zbuf[...] = jnp.zeros_like(zbuf)

        def body(z, c):
            zero_copy(zero_row0(z)).start()
            return c
        lax.fori_loop(0, n_zero, body, 0)

    def start_fetch(tab, slot):
        def body(g, c):
            for u in range(COPY_UNROLL):
                p = g * COPY_UNROLL + u
                pltpu.make_async_copy(
                    xl_hbm.at[pl.ds(pl.multiple_of(tab[p], GROUP), GROUP)],
                    xbuf.at[slot, pl.ds(pl.multiple_of(p * GROUP, GROUP), GROUP)],
                    sem_in.at[slot]).start()
            return c
        lax.fori_loop(0, PIECES // COPY_UNROLL, body, 0)

    def wait_fetch(slot):
        pltpu.make_async_copy(xl_hbm.at[pl.ds(0, BM)], xbuf.at[slot], sem_in.at[slot]).wait()

    def wait_store(slot):
        pltpu.make_async_copy(ybuf.at[slot], yl_hbm.at[pl.ds(0, BM)], sem_out.at[slot]).wait()

    @pl.when(i == 0)
    def _():
        start_fetch(tab_ref, 0)

    @pl.when(i + 1 < n_used)
    def _():
        start_fetch(tabn_ref, 1 - par)

    @pl.when(i < n_used)
    def _():
        wait_fetch(par)

        @pl.when(i >= 2)
        def _():
            wait_store(par)

        a = jnp.dot(xbuf[par].astype(BF16), w1_ref[0], preferred_element_type=F32) + b1_ref[0]
        a_glu = jnp.minimum(a[:, :EXPERT_FF], SWIGLU_LIMIT)
        a_lin = jnp.clip(a[:, EXPERT_FF:], -SWIGLU_LIMIT, SWIGLU_LIMIT)
        act = a_glu * jax.nn.sigmoid(SWIGLU_ALPHA * a_glu) * (a_lin + 1.0)
        ybuf[par] = jnp.dot(act.astype(BF16), w2_ref[0], preferred_element_type=F32) + b2_ref[0]

        @pl.when(i == 0)
        def _():
            def body(z, c):
                zero_copy(0).wait()
                return c
            lax.fori_loop(0, n_zero, body, 0)

        def store(g, c):
            for u in range(COPY_UNROLL):
                p = g * COPY_UNROLL + u
                pltpu.make_async_copy(
                    ybuf.at[par, pl.ds(pl.multiple_of(p * GROUP, GROUP), GROUP)],
                    yl_hbm.at[pl.ds(pl.multiple_of(tab_ref[PIECES + p], GROUP), GROUP)],
                    sem_out.at[par]).start()
            return c
        lax.fori_loop(0, PIECES // COPY_UNROLL, store, 0)

        @pl.when(i == n_used - 1)
        def _():
            wait_store(par)

            @pl.when(i >= 1)
            def _():
                wait_store(1 - par)


def _ffn_call(block_expert, n_used, piece_tab, xl, w1_b, b1, w2_b, b2):
    n_tiles, _, d = xl.shape
    n_blocks = piece_tab.shape[0] // TAB
    xl2 = xl.reshape(-1, d)
    yl_shape = (n_tiles + 1, LOCAL_ROWS, d)
    grid_spec = pltpu.PrefetchScalarGridSpec(
        num_scalar_prefetch=2,
        grid=(n_blocks,),
        in_specs=[
            pl.BlockSpec((TAB,), lambda i, be, nu: (i,), memory_space=pltpu.SMEM),
            pl.BlockSpec((TAB,), lambda i, be, nu: (jnp.minimum(i + 1, n_blocks - 1),),
                         memory_space=pltpu.SMEM),
            pl.BlockSpec((1, d, 2 * EXPERT_FF), lambda i, be, nu: (be[i], 0, 0)),
            pl.BlockSpec((1, 1, 2 * EXPERT_FF), lambda i, be, nu: (be[i], 0, 0)),
            pl.BlockSpec((1, EXPERT_FF, d), lambda i, be, nu: (be[i], 0, 0)),
            pl.BlockSpec((1, 1, d), lambda i, be, nu: (be[i], 0, 0)),
            pl.BlockSpec(memory_space=pl.ANY),
        ],
        out_specs=pl.BlockSpec(memory_space=pl.ANY),
        scratch_shapes=[pltpu.VMEM((2, BM, d), F32), pltpu.VMEM((2, BM, d), F32),
                        pltpu.VMEM((PAD_ROWS, d), F32),
                        pltpu.SemaphoreType.DMA((2,)), pltpu.SemaphoreType.DMA((2,)),
                        pltpu.SemaphoreType.DMA],
    )
    out = pl.pallas_call(
        _ffn_kernel,
        grid_spec=grid_spec,
        out_shape=jax.ShapeDtypeStruct((yl_shape[0] * LOCAL_ROWS, d), F32),
        compiler_params=pltpu.CompilerParams(
            dimension_semantics=("arbitrary",), vmem_limit_bytes=VMEM_LIMIT),
        name="moe_ffn",
    )(block_expert, n_used, piece_tab, piece_tab, w1_b, b1, w2_b, b2, xl2)
    return out.reshape(yl_shape)


def _combine_kernel(yl_ref, lpos_ref, w_ref, x1_ref, gt_ref, gfin_ref, o_ref):
    acc = jnp.dot(_unpermute_matrix(lpos_ref[...], w_ref[...]), yl_ref[0].astype(BF16),
                  preferred_element_type=F32)
    out = x1_ref[...] + gt_ref[0] * acc
    o_ref[...] = out * lax.rsqrt(jnp.mean(out * out, axis=-1, keepdims=True) + EPS) * gfin_ref[...]


def _combine_call(yl, lpos_tok, w_tok, x1, gtf, g_final, seq):
    n_tok, d = x1.shape
    per_b = seq // TM
    return pl.pallas_call(
        _combine_kernel,
        grid=(n_tok // TM,),
        in_specs=[
            pl.BlockSpec((1, LOCAL_ROWS, d), lambda i: (i, 0, 0)),
            pl.BlockSpec((TM, TOP_K), lambda i: (i, 0)),
            pl.BlockSpec((TM, TOP_K), lambda i: (i, 0)),
            pl.BlockSpec((TM, d), lambda i: (i, 0)),
            pl.BlockSpec((1, 1, d), lambda i: (i // per_b, 0, 0)),
            pl.BlockSpec((1, d), lambda i: (0, 0)),
        ],
        out_specs=pl.BlockSpec((TM, d), lambda i: (i, 0)),
        out_shape=jax.ShapeDtypeStruct((n_tok, d), F32),
        compiler_params=pltpu.CompilerParams(
            dimension_semantics=("arbitrary",), vmem_limit_bytes=VMEM_LIMIT),
        name="moe_combine",
    )(yl, lpos_tok, w_tok, x1, gtf, g_final)


def _tables(seq):
    half = HEAD_DIM // 2
    inv_freq = ROPE_BASE ** (-jnp.arange(half, dtype=F32) / half)
    ang = jnp.arange(seq, dtype=F32)[:, None] * inv_freq[None, :]
    cos = jnp.cos(ang)
    sin = jnp.sin(ang)
    cos_t = jnp.concatenate([cos, cos], axis=-1)
    sin_t = jnp.concatenate([-sin, sin], axis=-1)
    gamma = 1.0 - jnp.exp2(-5.0 - jnp.arange(N_HEADS, dtype=F32))
    log_g = jnp.log(gamma)
    idx = jnp.arange(CHUNK, dtype=F32)
    diff = idx[:, None] - idx[None, :]
    dmat = jnp.where(diff[None] >= 0,
                     jnp.exp(log_g[:, None, None] * jnp.maximum(diff, 0.0)[None]), 0.0)
    k_decay = jnp.exp(log_g[:, None] * (CHUNK - 1.0 - idx)[None, :])
    q_decay = jnp.exp(log_g[:, None] * (idx + 1.0)[None, :])
    chunk_decay = jnp.exp(log_g * CHUNK)
    widen = lambda t: jnp.repeat(t.T[:, :, None], HEAD_DIM, axis=2).reshape(CHUNK, GROUP_W)
    qdec = widen(q_decay)
    kdec = widen(k_decay)
    cdec = jnp.repeat(chunk_decay[:, None], HEAD_DIM, axis=1).reshape(1, GROUP_W)
    return cos_t, sin_t, dmat, qdec, kdec, cdec


def kernel(x, c, w_ada, b_ada, g_mix, w_in, gmlp_ln_g, gmlp_ln_b, w_spatial, b_spatial,
           ret_norm_g, w_out, g_ffn, w_router, b_router, w1, b1, w2, b2, g_final):
    bsz, seq, d = x.shape
    depth = w_in.shape[0]
    n_tok = bsz * seq
    n_tiles = n_tok // TM
    n_slots = n_tok * TOP_K + n_tiles * N_EXPERTS * GROUP + N_EXPERTS * BM
    assert n_slots % BM == 0
    n_blocks = n_slots // BM

    mod = _ada_call(c, w_ada, b_ada)
    cos_t, sin_t, dmat, qdec, kdec, cdec = _tables(seq)

    w_in_b = w_in.astype(BF16)
    w_out_b = w_out.astype(BF16)
    wr_t = jnp.swapaxes(w_router, 1, 2).astype(BF16)
    br = b_router.reshape(depth, N_EXPERTS, 1)
    w1_b = w1.astype(BF16).reshape(depth * N_EXPERTS, d, 2 * EXPERT_FF)
    w2_b = w2.astype(BF16).reshape(depth * N_EXPERTS, EXPERT_FF, d)
    b1_r = b1.reshape(depth * N_EXPERTS, 1, 2 * EXPERT_FF)
    b2_r = b2.reshape(depth * N_EXPERTS, 1, d)
    bs_full = jnp.repeat(jnp.swapaxes(b_spatial, 1, 2)[:, :, :, None], HEAD_DIM, axis=3)
    bs_full = bs_full.reshape(depth, CHUNK, GROUP_W)
    row3 = lambda a: a.reshape(depth, 1, a.shape[-1])
    gfin = g_final.reshape(1, d)
    e_ids = jnp.arange(N_EXPERTS, dtype=I32)
    blk = jnp.arange(n_blocks, dtype=I32)

    piece_off = GROUP * jnp.arange(PIECES, dtype=I32)
    tile_row0 = LOCAL_ROWS * jnp.arange(n_tiles, dtype=I32)
    zero_row = LOCAL_ROWS - GROUP
    spare_row0 = n_tiles * LOCAL_ROWS
    prev = None
    for l in range(depth):
        mods = [m.reshape(bsz, 1, d) for m in jnp.split(mod[l], N_MOD, axis=-1)]
        shm, scm, gtm, shf, scf, gtf = mods
        x1, xl, lpos8, wgt8, tab3, cnt = _mixer_call(
            l, x, (shm, scm, gtm, shf, scf), row3(g_mix), row3(g_ffn), w_in_b,
            row3(gmlp_ln_g), row3(gmlp_ln_b), w_spatial, bs_full, row3(ret_norm_g), w_out_b,
            cos_t, sin_t, dmat, qdec, kdec, cdec, wr_t, br, prev=prev)

        cntp = cnt[:, 0].astype(I32)
        padded = ((cntp + BM - 1) // BM) * BM
        pend = jnp.sum(jnp.where(e_ids[None, :] <= e_ids[:, None], padded[None, :], 0), axis=1)
        pstart = pend - padded
        n_used = pend[-1] // BM
        block_e = jnp.minimum(
            jnp.sum((pend[None, :] <= (blk * BM)[:, None]).astype(I32), axis=1), N_EXPERTS - 1)
        gbase, runs, lstart = tab3[:, 0, :, 0], tab3[:, 1, :, 0], tab3[:, 2, :, 0]
        r = (blk * BM - pstart[block_e])[:, None] + piece_off[None, :]
        gb = gbase.T[block_e]
        rn = runs.T[block_e]
        delta = tile_row0[None, :] + lstart.T[block_e] - gb
        in_run = (gb[:, None, :] <= r[:, :, None]) & (r[:, :, None] < (gb + rn)[:, None, :])
        row = r + jnp.sum(jnp.where(in_run, delta[:, None, :], 0), axis=-1)
        live = (r < cntp[block_e][:, None]) & (blk < n_used)[:, None]
        src = jnp.where(live, row, zero_row)
        dst = jnp.where(live, row, spare_row0 + (blk % 2)[:, None] * BM + piece_off[None, :])
        piece_tab = jnp.concatenate(
            [src, dst, jnp.zeros((n_blocks, TAB - 2 * PIECES), I32)], axis=1).reshape(-1)

        yl = _ffn_call(block_e + l * N_EXPERTS, n_used.reshape(1), piece_tab, xl,
                       w1_b, b1_r, w2_b, b2_r)
        x = x1
        prev = (yl, lpos8[:TOP_K].T, wgt8[:TOP_K].T, gtf)
    out = _combine_call(prev[0], prev[1], prev[2], x.reshape(n_tok, d), prev[3], gfin, seq)
    return out.reshape(bsz, seq, d)
```

```python
import functools

import jax
import jax.numpy as jnp
from jax import lax
from jax.experimental import pallas as pl
from jax.experimental.pallas import tpu as pltpu

F32 = jnp.float32
BF16 = jnp.bfloat16
I32 = jnp.int32

D_MODEL = 1024
CHUNK = 128
N_HEADS = 4
HEAD_DIM = 128
GROUP_W = N_HEADS * HEAD_DIM
N_EXPERTS = 32
TOP_K = 4
EXPERT_FF = 1024
N_MOD = 6
EPS = 1e-5
ROPE_BASE = 10000.0
SWIGLU_ALPHA = 1.702
SWIGLU_LIMIT = 7.0

TM = 256
BM = 512
GROUP = 8
PAD_ROWS = N_EXPERTS * GROUP
LOCAL_ROWS = TOP_K * TM + PAD_ROWS
PIECES = BM // GROUP
COPY_UNROLL = 8
TAB = 1024
VMEM_LIMIT = 56 * 1024 * 1024


def _gelu_tanh(x):
    return x * (0.5 * (1.0 + jnp.tanh(0.7978845608028654 * (x + 0.044715 * (x * x * x)))))


def _silu(x):
    return x * jax.nn.sigmoid(x)


def _ada_kernel(c_ref, w_ref, b_ref, o_ref):
    ca = _silu(c_ref[...]).astype(BF16)
    o_ref[0] = jnp.dot(ca, w_ref[0].astype(BF16), preferred_element_type=F32) + b_ref[0]


def _ada_call(c, w_ada, b_ada):
    depth, d, _ = w_ada.shape
    bsz = c.shape[0]
    return pl.pallas_call(
        _ada_kernel,
        grid=(depth, N_MOD),
        in_specs=[
            pl.BlockSpec((bsz, d), lambda l, n: (0, 0)),
            pl.BlockSpec((1, d, d), lambda l, n: (l, 0, n)),
            pl.BlockSpec((1, 1, d), lambda l, n: (l, 0, n)),
        ],
        out_specs=pl.BlockSpec((1, bsz, d), lambda l, n: (l, 0, n)),
        out_shape=jax.ShapeDtypeStruct((depth, bsz, N_MOD * d), F32),
        name="adaln_mod",
    )(c, w_ada, b_ada.reshape(depth, 1, N_MOD * d))


def _unpermute_matrix(lpos_tok, w_tok):
    l_io = lax.broadcasted_iota(I32, (TM, LOCAL_ROWS), 1)
    unperm = jnp.where(l_io == lpos_tok[:, 0:1], w_tok[:, 0:1], 0.0)
    for kk in range(1, TOP_K):
        unperm = unperm + jnp.where(l_io == lpos_tok[:, kk:kk + 1], w_tok[:, kk:kk + 1], 0.0)
    return unperm.astype(BF16)


def _permute_matrix(lpos_rows):
    lr_io = lax.broadcasted_iota(I32, (LOCAL_ROWS, TM), 0)
    hit = lr_io == lpos_rows[0:1]
    for kk in range(1, TOP_K):
        hit = hit | (lr_io == lpos_rows[kk:kk + 1])
    return jnp.where(hit, 1.0, 0.0).astype(BF16)


def _mixer_kernel(*refs, fuse_prev, nj, n_tiles):
    if fuse_prev:
        (ylp_ref, lposp_ref, wp_ref, lposn_ref, wn_ref, gtp_ref), refs = refs[:6], refs[6:]
        unperm_sc, refs = refs[-1], refs[:-1]
    (x_ref, shm_ref, scm_ref, gtm_ref, shf_ref, scf_ref, gmix_ref, gffn_ref,
     win_ref, lng_ref, lnb_ref, ws_ref, bs_ref, rng_ref, wout_ref,
     cos_ref, sin_ref, dmat_ref, qdec_ref, kdec_ref, cdec_ref, wr_ref, br_ref,
     x1_ref, xl_ref, lpos_ref, wgt_ref, tab_ref, cnt_ref,
     state_ref, base_ref, ob_ref, h2_sc, lpos_sc) = refs
    t = pl.program_id(0)
    live = t < n_tiles
    j = jnp.minimum(t, n_tiles - 1) % nj
    n_chunks = TM // CHUNK

    @pl.when(j == 0)
    def _():
        state_ref[...] = jnp.zeros_like(state_ref)

    @pl.when(t == 0)
    def _():
        base_ref[...] = jnp.zeros_like(base_ref)
        h2_sc[...] = jnp.zeros_like(h2_sc)
        lpos_sc[...] = jnp.zeros_like(lpos_sc)
        if fuse_prev:
            unperm_sc[...] = _unpermute_matrix(lposp_ref[...], wp_ref[...])

    x = x_ref[0]
    if fuse_prev:
        x = x + gtp_ref[0] * jnp.dot(unperm_sc[...], ylp_ref[0].astype(BF16),
                                     preferred_element_type=F32)

    ms = jnp.mean(x * x, axis=-1, keepdims=True)
    h = x * lax.rsqrt(ms + EPS) * (gmix_ref[0] * (1.0 + scm_ref[0])) + shm_ref[0]
    hb = h.astype(BF16)

    def proj(k):
        return jnp.dot(hb, win_ref[0, :, k * GROUP_W:(k + 1) * GROUP_W],
                       preferred_element_type=F32)

    def rows(c):
        return slice(c * CHUNK, (c + 1) * CHUNK)

    def lanes(hh):
        return slice(hh * HEAD_DIM, (hh + 1) * HEAD_DIM)

    gu = _gelu_tanh(proj(0))
    gv = _gelu_tanh(proj(1))
    mu = jnp.mean(gv, axis=-1, keepdims=True)
    vc = gv - mu
    var = jnp.mean(vc * vc, axis=-1, keepdims=True)
    vn = (vc * lax.rsqrt(var + EPS) * lng_ref[0] + lnb_ref[0]).astype(BF16)
    t_io = lax.broadcasted_iota(I32, (CHUNK, CHUNK), 0)
    s_io = lax.broadcasted_iota(I32, (CHUNK, CHUNK), 1)
    causal = t_io >= s_io
    bias = bs_ref[0]
    for hh in range(N_HEADS):
        wm = jnp.where(causal, ws_ref[0, hh], 0.0).astype(BF16)
        for c in range(n_chunks):
            mixed = jnp.dot(wm, vn[rows(c), lanes(hh)], preferred_element_type=F32)
            oa = gu[rows(c), lanes(hh)] * (mixed + bias[:, lanes(hh)])
            ob_ref[rows(c), lanes(hh)] = oa.astype(BF16)

    q = proj(2)
    k = proj(3)
    vr = proj(4).astype(BF16)
    g = proj(5)
    cos = cos_ref[...]
    sin = sin_ref[...]
    k_scale = HEAD_DIM ** -0.5
    for hh in range(N_HEADS):
        qh = q[:, lanes(hh)]
        kh = k[:, lanes(hh)]
        qh = qh * cos + pltpu.roll(qh, HEAD_DIM // 2, 1) * sin
        kh = (kh * cos + pltpu.roll(kh, HEAD_DIM // 2, 1) * sin) * k_scale
        dm = dmat_ref[hh]
        qd = qdec_ref[:, lanes(hh)]
        kd = kdec_ref[:, lanes(hh)]
        cd = cdec_ref[:, lanes(hh)]
        gn = rng_ref[0][:, lanes(hh)]
        st = state_ref[hh]
        for c in range(n_chunks):
            qc = qh[rows(c)]
            kc = kh[rows(c)]
            vcb = vr[rows(c), lanes(hh)]
            s = lax.dot_general(qc.astype(BF16), kc.astype(BF16),
                                (((1,), (1,)), ((), ())), preferred_element_type=F32) * dm
            o = jnp.dot(s.astype(BF16), vcb, preferred_element_type=F32)
            o = o + jnp.dot((qc * qd).astype(BF16), st.astype(BF16), preferred_element_type=F32)
            kdt = (kc * kd).T.astype(BF16)
            st = st * cd + jnp.dot(kdt, vcb, preferred_element_type=F32)
            o = o * lax.rsqrt(jnp.mean(o * o, axis=-1, keepdims=True) + EPS) * gn
            ob = o * _silu(g[rows(c), lanes(hh)])
            ob_ref[rows(c), GROUP_W + hh * HEAD_DIM:GROUP_W + (hh + 1) * HEAD_DIM] = ob.astype(BF16)
        state_ref[hh] = st

    y = jnp.dot(ob_ref[...], wout_ref[0], preferred_element_type=F32)
    x1 = x + gtm_ref[0] * y

    ms2 = jnp.mean(x1 * x1, axis=-1, keepdims=True)
    h2 = x1 * lax.rsqrt(ms2 + EPS) * (gffn_ref[0] * (1.0 + scf_ref[0])) + shf_ref[0]
    h2b = h2.astype(BF16)

    if fuse_prev:
        unperm_sc[...] = _unpermute_matrix(lposn_ref[...], wn_ref[...])

    lg = lax.dot_general(wr_ref[0], h2b, (((1,), (1,)), ((), ())),
                         preferred_element_type=F32) + br_ref[0]
    e_io = lax.broadcasted_iota(I32, (N_EXPERTS, TM), 0)
    work = lg
    vals, ids = [], []
    for _ in range(TOP_K):
        m = jnp.max(work, axis=0, keepdims=True)
        ik = jnp.min(jnp.where(work == m, e_io, N_EXPERTS), axis=0, keepdims=True)
        vals.append(m)
        ids.append(ik)
        work = jnp.where(e_io == ik, -jnp.inf, work)
    exps = [jnp.exp(v - vals[0]) for v in vals]
    denom = exps[0] + exps[1] + exps[2] + exps[3]
    wts = [e / denom for e in exps]

    xl_ref[0] = jnp.dot(_permute_matrix(lpos_sc[...]), h2_sc[...], preferred_element_type=F32)

    sel = [e_io == ik for ik in ids]
    oh = jnp.where(sel[0] | sel[1] | sel[2] | sel[3], 1.0, 0.0)
    r_io = lax.broadcasted_iota(I32, (TM, TM), 0)
    c_io = lax.broadcasted_iota(I32, (TM, TM), 1)
    upper = jnp.where(r_io < c_io, 1.0, 0.0).astype(BF16)
    cum = jnp.dot(oh.astype(BF16), upper, preferred_element_type=F32)
    cnt_t = jnp.sum(oh, axis=1, keepdims=True)
    run = jnp.ceil(cnt_t * (1.0 / GROUP)) * GROUP
    run_b = jnp.broadcast_to(run, (N_EXPERTS, TM))
    er_io = lax.broadcasted_iota(I32, (N_EXPERTS, N_EXPERTS), 0)
    ec_io = lax.broadcasted_iota(I32, (N_EXPERTS, N_EXPERTS), 1)
    lower = jnp.where(ec_io < er_io, 1.0, 0.0).astype(BF16)
    lstart = jnp.dot(lower, run_b.astype(BF16), preferred_element_type=F32)
    pos = lstart + cum
    lpos = [jnp.sum(jnp.where(sk, pos, 0.0), axis=0, keepdims=True) for sk in sel]
    gbase = base_ref[...]
    row8 = lax.broadcasted_iota(I32, (8, TM), 0)

    def stack4(parts, fill):
        out = jnp.full((8, TM), fill, parts[0].dtype)
        for kk in range(TOP_K):
            out = jnp.where(row8 == kk, parts[kk], out)
        return out

    lpos8 = stack4([p.astype(I32) for p in lpos], 0)
    wgt8 = stack4(wts, 0.0)

    @pl.when(live)
    def _():
        x1_ref[0] = x1
        base_ref[...] = gbase + run_b
        cnt_ref[...] = (gbase + run_b)[:, :128]
        tab_ref[0, 0] = gbase[:, :128].astype(I32)
        tab_ref[0, 1] = run_b[:, :128].astype(I32)
        tab_ref[0, 2] = lstart[:, :128].astype(I32)
        lpos_ref[...] = lpos8
        wgt_ref[...] = wgt8
        h2_sc[...] = h2b
        lpos_sc[...] = lpos8


def _mixer_call(layer, x, mods, g_mix, g_ffn, w_in_b, ln_g, ln_b, w_sp, bs_full, rn_g,
                w_out_b, cos_t, sin_t, dmat, qdec, kdec, cdec, wr_t, br, prev=None):
    bsz, seq, d = x.shape
    n_tok = bsz * seq
    nj = seq // TM
    n_tiles = bsz * nj
    shm, scm, gtm, shf, scf = mods
    tile = lambda t: jnp.minimum(t, n_tiles - 1)
    vec = lambda: pl.BlockSpec((1, 1, d), lambda t: (tile(t) // nj, 0, 0))
    lvec = lambda w: pl.BlockSpec((1, 1, w), lambda t: (layer, 0, 0))
    full = lambda shp: pl.BlockSpec(shp, lambda t: (0,) * len(shp))
    tok = lambda shp: pl.BlockSpec(shp, lambda t: (0, tile(t)))
    in_specs = [
        pl.BlockSpec((1, TM, d), lambda t: (tile(t) // nj, tile(t) % nj, 0)),
        vec(), vec(), vec(), vec(), vec(),
        lvec(d), lvec(d),
        pl.BlockSpec((1, d, 6 * GROUP_W), lambda t: (layer, 0, 0)),
        lvec(GROUP_W), lvec(GROUP_W),
        pl.BlockSpec((1, N_HEADS, CHUNK, CHUNK), lambda t: (layer, 0, 0, 0)),
        pl.BlockSpec((1, CHUNK, GROUP_W), lambda t: (layer, 0, 0)),
        lvec(GROUP_W),
        pl.BlockSpec((1, d, d), lambda t: (layer, 0, 0)),
        pl.BlockSpec((TM, HEAD_DIM), lambda t: (tile(t) % nj, 0)),
        pl.BlockSpec((TM, HEAD_DIM), lambda t: (tile(t) % nj, 0)),
        full((N_HEADS, CHUNK, CHUNK)),
        full((CHUNK, GROUP_W)), full((CHUNK, GROUP_W)), full((1, GROUP_W)),
        pl.BlockSpec((1, N_EXPERTS, d), lambda t: (layer, 0, 0)),
        pl.BlockSpec((1, N_EXPERTS, 1), lambda t: (layer, 0, 0)),
    ]
    out_specs = [
        pl.BlockSpec((1, TM, d), lambda t: (tile(t) // nj, tile(t) % nj, 0)),
        pl.BlockSpec((1, LOCAL_ROWS, d), lambda t: (jnp.maximum(t - 1, 0), 0, 0)),
        tok((8, TM)), tok((8, TM)),
        pl.BlockSpec((1, 3, N_EXPERTS, 128), lambda t: (tile(t), 0, 0, 0)),
        pl.BlockSpec((N_EXPERTS, 128), lambda t: (0, 0)),
    ]
    out_shape = [
        jax.ShapeDtypeStruct((bsz, seq, d), F32),
        jax.ShapeDtypeStruct((n_tiles, LOCAL_ROWS, d), F32),
        jax.ShapeDtypeStruct((8, n_tok), I32),
        jax.ShapeDtypeStruct((8, n_tok), F32),
        jax.ShapeDtypeStruct((n_tiles, 3, N_EXPERTS, 128), I32),
        jax.ShapeDtypeStruct((N_EXPERTS, 128), F32),
    ]
    args = (x, shm, scm, gtm, shf, scf, g_mix, g_ffn, w_in_b, ln_g, ln_b, w_sp, bs_full, rn_g,
            w_out_b, cos_t, sin_t, dmat, qdec, kdec, cdec, wr_t, br)
    scratch = [
        pltpu.VMEM((N_HEADS, HEAD_DIM, HEAD_DIM), F32),
        pltpu.VMEM((N_EXPERTS, TM), F32),
        pltpu.VMEM((TM, d), BF16),
        pltpu.VMEM((TM, d), BF16),
        pltpu.VMEM((8, TM), I32),
    ]
    if prev is not None:
        yl, lpos_tok, w_tok, gtf = prev
        nxt = lambda t: jnp.minimum(t + 1, n_tiles - 1)
        in_specs = [
            pl.BlockSpec((1, LOCAL_ROWS, d), lambda t: (tile(t), 0, 0)),
            pl.BlockSpec((TM, TOP_K), lambda t: (tile(t), 0)),
            pl.BlockSpec((TM, TOP_K), lambda t: (tile(t), 0)),
            pl.BlockSpec((TM, TOP_K), lambda t: (nxt(t), 0)),
            pl.BlockSpec((TM, TOP_K), lambda t: (nxt(t), 0)),
            vec(),
        ] + in_specs
        args = (yl, lpos_tok, w_tok, lpos_tok, w_tok, gtf) + args
        scratch.append(pltpu.VMEM((TM, LOCAL_ROWS), BF16))
    return pl.pallas_call(
        functools.partial(_mixer_kernel, fuse_prev=prev is not None, nj=nj, n_tiles=n_tiles),
        grid=(n_tiles + 1,),
        in_specs=in_specs,
        out_specs=out_specs,
        out_shape=out_shape,
        scratch_shapes=scratch,
        compiler_params=pltpu.CompilerParams(
            dimension_semantics=("arbitrary",), vmem_limit_bytes=VMEM_LIMIT),
        name="mixer_router",
    )(*args)


def _ffn_kernel(be_ref, nused_ref, tab_ref, tabn_ref, w1_ref, b1_ref, w2_ref, b2_ref,
                xl_hbm, yl_hbm, xbuf, ybuf, zbuf, w1b, w2b, sem_in, sem_out, zsem):
    i = pl.program_id(0)
    par = i % 2
    n_used = nused_ref[0]
    n_tiles = yl_hbm.shape[0] // LOCAL_ROWS - 1

    @pl.when((i < n_used) & ((i == 0) | (be_ref[i] != be_ref[jnp.maximum(i - 1, 0)])))
    def _():
        w1b[...] = w1_ref[0].astype(BF16)
        w2b[...] = w2_ref[0].astype(BF16)

    def zero_copy(row0):
        return pltpu.make_async_copy(
            zbuf, yl_hbm.at[pl.ds(pl.multiple_of(row0, PAD_ROWS), PAD_ROWS)], zsem)

    n_zero = n_tiles + LOCAL_ROWS // PAD_ROWS

    def zero_row0(z):
        return jnp.where(z < n_tiles, z * LOCAL_ROWS + TOP_K * TM,
                         n_tiles * LOCAL_ROWS + (z - n_tiles) * PAD_ROWS)

    @pl.when(i == 0)
    def _():
        zbuf[...] = jnp.zeros_like(zbuf)

        def body(z, c):
            zero_copy(zero_row0(z)).start()
            return c
        lax.fori_loop(0, n_zero, body, 0)

    def start_fetch(tab, slot):
        def body(g, c):
            for u in range(COPY_UNROLL):
                p = g * COPY_UNROLL + u
                pltpu.make_async_copy(
                    xl_hbm.at[pl.ds(pl.multiple_of(tab[p], GROUP), GROUP)],
                    xbuf.at[slot, pl.ds(pl.multiple_of(p * GROUP, GROUP), GROUP)],
                    sem_in.at[slot]).start()
            return c
        lax.fori_loop(0, PIECES // COPY_UNROLL, body, 0)

    def wait_fetch(slot):
        pltpu.make_async_copy(xl_hbm.at[pl.ds(0, BM)], xbuf.at[slot], sem_in.at[slot]).wait()

    def wait_store(slot):
        pltpu.make_async_copy(ybuf.at[slot], yl_hbm.at[pl.ds(0, BM)], sem_out.at[slot]).wait()

    @pl.when(i == 0)
    def _():
        start_fetch(tab_ref, 0)

    @pl.when(i + 1 < n_used)
    def _():
        start_fetch(tabn_ref, 1 - par)

    @pl.when(i < n_used)
    def _():
        wait_fetch(par)

        @pl.when(i >= 2)
        def _():
            wait_store(par)

        a = jnp.dot(xbuf[par].astype(BF16), w1b[...], preferred_element_type=F32) + b1_ref[0]
        a_glu = jnp.minimum(a[:, :EXPERT_FF], SWIGLU_LIMIT)
        a_lin = jnp.clip(a[:, EXPERT_FF:], -SWIGLU_LIMIT, SWIGLU_LIMIT)
        act = a_glu * jax.nn.sigmoid(SWIGLU_ALPHA * a_glu) * (a_lin + 1.0)
        ybuf[par] = jnp.dot(act.astype(BF16), w2b[...], preferred_element_type=F32) + b2_ref[0]

        @pl.when(i == 0)
        def _():
            def body(z, c):
                zero_copy(0).wait()
                return c
            lax.fori_loop(0, n_zero, body, 0)

        def store(g, c):
            for u in range(COPY_UNROLL):
                p = g * COPY_UNROLL + u
                pltpu.make_async_copy(
                    ybuf.at[par, pl.ds(pl.multiple_of(p * GROUP, GROUP), GROUP)],
                    yl_hbm.at[pl.ds(pl.multiple_of(tab_ref[PIECES + p], GROUP), GROUP)],
                    sem_out.at[par]).start()
            return c
        lax.fori_loop(0, PIECES // COPY_UNROLL, store, 0)

        @pl.when(i == n_used - 1)
        def _():
            wait_store(par)

            @pl.when(i >= 1)
            def _():
                wait_store(1 - par)


def _ffn_call(block_expert, n_used, piece_tab, xl, w1, b1, w2, b2):
    n_tiles, _, d = xl.shape
    n_blocks = piece_tab.shape[0] // TAB
    xl2 = xl.reshape(-1, d)
    yl_shape = (n_tiles + 1, LOCAL_ROWS, d)
    grid_spec = pltpu.PrefetchScalarGridSpec(
        num_scalar_prefetch=2,
        grid=(n_blocks,),
        in_specs=[
            pl.BlockSpec((TAB,), lambda i, be, nu: (i,), memory_space=pltpu.SMEM),
            pl.BlockSpec((TAB,), lambda i, be, nu: (jnp.minimum(i + 1, n_blocks - 1),),
                         memory_space=pltpu.SMEM),
            pl.BlockSpec((1, d, 2 * EXPERT_FF), lambda i, be, nu: (be[i], 0, 0)),
            pl.BlockSpec((1, 1, 2 * EXPERT_FF), lambda i, be, nu: (be[i], 0, 0)),
            pl.BlockSpec((1, EXPERT_FF, d), lambda i, be, nu: (be[i], 0, 0)),
            pl.BlockSpec((1, 1, d), lambda i, be, nu: (be[i], 0, 0)),
            pl.BlockSpec(memory_space=pl.ANY),
        ],
        out_specs=pl.BlockSpec(memory_space=pl.ANY),
        scratch_shapes=[pltpu.VMEM((2, BM, d), F32), pltpu.VMEM((2, BM, d), F32),
                        pltpu.VMEM((PAD_ROWS, d), F32),
                        pltpu.VMEM((d, 2 * EXPERT_FF), BF16), pltpu.VMEM((EXPERT_FF, d), BF16),
                        pltpu.SemaphoreType.DMA((2,)), pltpu.SemaphoreType.DMA((2,)),
                        pltpu.SemaphoreType.DMA],
    )
    out = pl.pallas_call(
        _ffn_kernel,
        grid_spec=grid_spec,
        out_shape=jax.ShapeDtypeStruct((yl_shape[0] * LOCAL_ROWS, d), F32),
        compiler_params=pltpu.CompilerParams(
            dimension_semantics=("arbitrary",), vmem_limit_bytes=VMEM_LIMIT),
        name="moe_ffn",
    )(block_expert, n_used, piece_tab, piece_tab, w1, b1, w2, b2, xl2)
    return out.reshape(yl_shape)


def _combine_kernel(yl_ref, lpos_ref, w_ref, x1_ref, gt_ref, gfin_ref, o_ref):
    acc = jnp.dot(_unpermute_matrix(lpos_ref[...], w_ref[...]), yl_ref[0].astype(BF16),
                  preferred_element_type=F32)
    out = x1_ref[...] + gt_ref[0] * acc
    o_ref[...] = out * lax.rsqrt(jnp.mean(out * out, axis=-1, keepdims=True) + EPS) * gfin_ref[...]


def _combine_call(yl, lpos_tok, w_tok, x1, gtf, g_final, seq):
    n_tok, d = x1.shape
    per_b = seq // TM
    return pl.pallas_call(
        _combine_kernel,
        grid=(n_tok // TM,),
        in_specs=[
            pl.BlockSpec((1, LOCAL_ROWS, d), lambda i: (i, 0, 0)),
            pl.BlockSpec((TM, TOP_K), lambda i: (i, 0)),
            pl.BlockSpec((TM, TOP_K), lambda i: (i, 0)),
            pl.BlockSpec((TM, d), lambda i: (i, 0)),
            pl.BlockSpec((1, 1, d), lambda i: (i // per_b, 0, 0)),
            pl.BlockSpec((1, d), lambda i: (0, 0)),
        ],
        out_specs=pl.BlockSpec((TM, d), lambda i: (i, 0)),
        out_shape=jax.ShapeDtypeStruct((n_tok, d), F32),
        compiler_params=pltpu.CompilerParams(
            dimension_semantics=("arbitrary",), vmem_limit_bytes=VMEM_LIMIT),
        name="moe_combine",
    )(yl, lpos_tok, w_tok, x1, gtf, g_final)


def _tables(seq):
    half = HEAD_DIM // 2
    inv_freq = ROPE_BASE ** (-jnp.arange(half, dtype=F32) / half)
    ang = jnp.arange(seq, dtype=F32)[:, None] * inv_freq[None, :]
    cos = jnp.cos(ang)
    sin = jnp.sin(ang)
    cos_t = jnp.concatenate([cos, cos], axis=-1)
    sin_t = jnp.concatenate([-sin, sin], axis=-1)
    gamma = 1.0 - jnp.exp2(-5.0 - jnp.arange(N_HEADS, dtype=F32))
    log_g = jnp.log(gamma)
    idx = jnp.arange(CHUNK, dtype=F32)
    diff = idx[:, None] - idx[None, :]
    dmat = jnp.where(diff[None] >= 0,
                     jnp.exp(log_g[:, None, None] * jnp.maximum(diff, 0.0)[None]), 0.0)
    k_decay = jnp.exp(log_g[:, None] * (CHUNK - 1.0 - idx)[None, :])
    q_decay = jnp.exp(log_g[:, None] * (idx + 1.0)[None, :])
    chunk_decay = jnp.exp(log_g * CHUNK)
    widen = lambda t: jnp.repeat(t.T[:, :, None], HEAD_DIM, axis=2).reshape(CHUNK, GROUP_W)
    qdec = widen(q_decay)
    kdec = widen(k_decay)
    cdec = jnp.repeat(chunk_decay[:, None], HEAD_DIM, axis=1).reshape(1, GROUP_W)
    return cos_t, sin_t, dmat, qdec, kdec, cdec


def kernel(x, c, w_ada, b_ada, g_mix, w_in, gmlp_ln_g, gmlp_ln_b, w_spatial, b_spatial,
           ret_norm_g, w_out, g_ffn, w_router, b_router, w1, b1, w2, b2, g_final):
    bsz, seq, d = x.shape
    depth = w_in.shape[0]
    n_tok = bsz * seq
    n_tiles = n_tok // TM
    n_slots = n_tok * TOP_K + n_tiles * N_EXPERTS * GROUP + N_EXPERTS * BM
    assert n_slots % BM == 0
    n_blocks = n_slots // BM

    mod = _ada_call(c, w_ada, b_ada)
    cos_t, sin_t, dmat, qdec, kdec, cdec = _tables(seq)

    w_in_b = w_in.astype(BF16)
    w_out_b = w_out.astype(BF16)
    wr_t = jnp.swapaxes(w_router, 1, 2).astype(BF16)
    br = b_router.reshape(depth, N_EXPERTS, 1)
    w1_r = w1.reshape(depth * N_EXPERTS, d, 2 * EXPERT_FF)
    w2_r = w2.reshape(depth * N_EXPERTS, EXPERT_FF, d)
    b1_r = b1.reshape(depth * N_EXPERTS, 1, 2 * EXPERT_FF)
    b2_r = b2.reshape(depth * N_EXPERTS, 1, d)
    bs_full = jnp.repeat(jnp.swapaxes(b_spatial, 1, 2)[:, :, :, None], HEAD_DIM, axis=3)
    bs_full = bs_full.reshape(depth, CHUNK, GROUP_W)
    row3 = lambda a: a.reshape(depth, 1, a.shape[-1])
    gfin = g_final.reshape(1, d)
    e_ids = jnp.arange(N_EXPERTS, dtype=I32)
    blk = jnp.arange(n_blocks, dtype=I32)

    piece_off = GROUP * jnp.arange(PIECES, dtype=I32)
    tile_row0 = LOCAL_ROWS * jnp.arange(n_tiles, dtype=I32)
    zero_row = LOCAL_ROWS - GROUP
    spare_row0 = n_tiles * LOCAL_ROWS
    prev = None
    for l in range(depth):
        mods = [m.reshape(bsz, 1, d) for m in jnp.split(mod[l], N_MOD, axis=-1)]
        shm, scm, gtm, shf, scf, gtf = mods
        x1, xl, lpos8, wgt8, tab3, cnt = _mixer_call(
            l, x, (shm, scm, gtm, shf, scf), row3(g_mix), row3(g_ffn), w_in_b,
            row3(gmlp_ln_g), row3(gmlp_ln_b), w_spatial, bs_full, row3(ret_norm_g), w_out_b,
            cos_t, sin_t, dmat, qdec, kdec, cdec, wr_t, br, prev=prev)

        cntp = cnt[:, 0].astype(I32)
        padded = ((cntp + BM - 1) // BM) * BM
        pend = jnp.sum(jnp.where(e_ids[None, :] <= e_ids[:, None], padded[None, :], 0), axis=1)
        pstart = pend - padded
        n_used = pend[-1] // BM
        block_e = jnp.minimum(
            jnp.sum((pend[None, :] <= (blk * BM)[:, None]).astype(I32), axis=1), N_EXPERTS - 1)
        gbase, runs, lstart = tab3[:, 0, :, 0], tab3[:, 1, :, 0], tab3[:, 2, :, 0]
        r = (blk * BM - pstart[block_e])[:, None] + piece_off[None, :]
        gb = gbase.T[block_e]
        rn = runs.T[block_e]
        delta = tile_row0[None, :] + lstart.T[block_e] - gb
        in_run = (gb[:, None, :] <= r[:, :, None]) & (r[:, :, None] < (gb + rn)[:, None, :])
        row = r + jnp.sum(jnp.where(in_run, delta[:, None, :], 0), axis=-1)
        live = (r < cntp[block_e][:, None]) & (blk < n_used)[:, None]
        src = jnp.where(live, row, zero_row)
        dst = jnp.where(live, row, spare_row0 + (blk % 2)[:, None] * BM + piece_off[None, :])
        piece_tab = jnp.concatenate(
            [src, dst, jnp.zeros((n_blocks, TAB - 2 * PIECES), I32)], axis=1).reshape(-1)

        yl = _ffn_call(block_e + l * N_EXPERTS, n_used.reshape(1), piece_tab, xl,
                       w1_r, b1_r, w2_r, b2_r)
        x = x1
        prev = (yl, lpos8[:TOP_K].T, wgt8[:TOP_K].T, gtf)
    out = _combine_call(prev[0], prev[1], prev[2], x.reshape(n_tok, d), prev[3], gfin, seq)
    return out.reshape(bsz, seq, d)
```

```python
import functools

import jax
import jax.numpy as jnp
from jax import lax
from jax.experimental import pallas as pl
from jax.experimental.pallas import tpu as pltpu

F32 = jnp.float32
BF16 = jnp.bfloat16
I32 = jnp.int32

D_MODEL = 1024
CHUNK = 128
N_HEADS = 4
HEAD_DIM = 128
GROUP_W = N_HEADS * HEAD_DIM
N_EXPERTS = 32
TOP_K = 4
EXPERT_FF = 1024
N_MOD = 6
EPS = 1e-5
ROPE_BASE = 10000.0
SWIGLU_ALPHA = 1.702
SWIGLU_LIMIT = 7.0

TM = 256
BM = 512
GROUP = 8
PAD_ROWS = N_EXPERTS * GROUP
LOCAL_ROWS = TOP_K * TM + PAD_ROWS
PIECES = BM // GROUP
COPY_UNROLL = 8
TAB = 1024
VMEM_LIMIT = 56 * 1024 * 1024


def _gelu_tanh(x):
    return x * (0.5 * (1.0 + jnp.tanh(0.7978845608028654 * (x + 0.044715 * (x * x * x)))))


def _silu(x):
    return x * jax.nn.sigmoid(x)


def _ada_kernel(c_ref, w_ref, b_ref, o_ref):
    ca = _silu(c_ref[...]).astype(BF16)
    o_ref[0] = jnp.dot(ca, w_ref[0].astype(BF16), preferred_element_type=F32) + b_ref[0]


def _ada_call(c, w_ada, b_ada):
    depth, d, _ = w_ada.shape
    bsz = c.shape[0]
    return pl.pallas_call(
        _ada_kernel,
        grid=(depth, N_MOD),
        in_specs=[
            pl.BlockSpec((bsz, d), lambda l, n: (0, 0)),
            pl.BlockSpec((1, d, d), lambda l, n: (l, 0, n)),
            pl.BlockSpec((1, 1, d), lambda l, n: (l, 0, n)),
        ],
        out_specs=pl.BlockSpec((1, bsz, d), lambda l, n: (l, 0, n)),
        out_shape=jax.ShapeDtypeStruct((depth, bsz, N_MOD * d), F32),
        name="adaln_mod",
    )(c, w_ada, b_ada.reshape(depth, 1, N_MOD * d))


def _unpermute_matrix(lpos_tok, w_tok):
    l_io = lax.broadcasted_iota(I32, (TM, LOCAL_ROWS), 1)
    unperm = jnp.where(l_io == lpos_tok[:, 0:1], w_tok[:, 0:1], 0.0)
    for kk in range(1, TOP_K):
        unperm = unperm + jnp.where(l_io == lpos_tok[:, kk:kk + 1], w_tok[:, kk:kk + 1], 0.0)
    return unperm.astype(BF16)


def _permute_matrix(lpos_rows):
    lr_io = lax.broadcasted_iota(I32, (LOCAL_ROWS, TM), 0)
    hit = lr_io == lpos_rows[0:1]
    for kk in range(1, TOP_K):
        hit = hit | (lr_io == lpos_rows[kk:kk + 1])
    return jnp.where(hit, 1.0, 0.0).astype(BF16)


def _mixer_kernel(*refs, fuse_prev, nj, n_tiles):
    if fuse_prev:
        (ylp_ref, lposp_ref, wp_ref, lposn_ref, wn_ref, gtp_ref), refs = refs[:6], refs[6:]
        unperm_sc, refs = refs[-1], refs[:-1]
    (x_ref, shm_ref, scm_ref, gtm_ref, shf_ref, scf_ref, gmix_ref, gffn_ref,
     win_ref, lng_ref, lnb_ref, ws_ref, bs_ref, rng_ref, wout_ref,
     cos_ref, sin_ref, dmat_ref, qdec_ref, kdec_ref, cdec_ref, wr_ref, br_ref,
     x1_ref, xl_ref, lpos_ref, wgt_ref, tab_ref, cnt_ref,
     state_ref, base_ref, ob_ref, h2_sc, lpos_sc) = refs
    t = pl.program_id(0)
    live = t < n_tiles
    j = jnp.minimum(t, n_tiles - 1) % nj
    n_chunks = TM // CHUNK

    @pl.when(j == 0)
    def _():
        state_ref[...] = jnp.zeros_like(state_ref)

    @pl.when(t == 0)
    def _():
        base_ref[...] = jnp.zeros_like(base_ref)
        h2_sc[...] = jnp.zeros_like(h2_sc)
        lpos_sc[...] = jnp.zeros_like(lpos_sc)
        if fuse_prev:
            unperm_sc[...] = _unpermute_matrix(lposp_ref[...], wp_ref[...])

    x = x_ref[0]
    if fuse_prev:
        x = x + gtp_ref[0] * jnp.dot(unperm_sc[...], ylp_ref[0].astype(BF16),
                                     preferred_element_type=F32)

    ms = jnp.mean(x * x, axis=-1, keepdims=True)
    h = x * lax.rsqrt(ms + EPS) * (gmix_ref[0] * (1.0 + scm_ref[0])) + shm_ref[0]
    hb = h.astype(BF16)

    def proj(k):
        return jnp.dot(hb, win_ref[0, :, k * GROUP_W:(k + 1) * GROUP_W],
                       preferred_element_type=F32)

    def rows(c):
        return slice(c * CHUNK, (c + 1) * CHUNK)

    def lanes(hh):
        return slice(hh * HEAD_DIM, (hh + 1) * HEAD_DIM)

    gu = _gelu_tanh(proj(0))
    gv = _gelu_tanh(proj(1))
    mu = jnp.mean(gv, axis=-1, keepdims=True)
    vc = gv - mu
    var = jnp.mean(vc * vc, axis=-1, keepdims=True)
    vn = (vc * lax.rsqrt(var + EPS) * lng_ref[0] + lnb_ref[0]).astype(BF16)
    t_io = lax.broadcasted_iota(I32, (CHUNK, CHUNK), 0)
    s_io = lax.broadcasted_iota(I32, (CHUNK, CHUNK), 1)
    causal = t_io >= s_io
    bias = bs_ref[0]
    for hh in range(N_HEADS):
        wm = jnp.where(causal, ws_ref[0, hh], 0.0).astype(BF16)
        for c in range(n_chunks):
            mixed = jnp.dot(wm, vn[rows(c), lanes(hh)], preferred_element_type=F32)
            oa = gu[rows(c), lanes(hh)] * (mixed + bias[:, lanes(hh)])
            ob_ref[rows(c), lanes(hh)] = oa.astype(BF16)

    q = proj(2)
    k = proj(3)
    vr = proj(4).astype(BF16)
    g = proj(5)
    cos = cos_ref[...]
    sin = sin_ref[...]
    k_scale = HEAD_DIM ** -0.5
    for hh in range(N_HEADS):
        qh = q[:, lanes(hh)]
        kh = k[:, lanes(hh)]
        qh = qh * cos + pltpu.roll(qh, HEAD_DIM // 2, 1) * sin
        kh = (kh * cos + pltpu.roll(kh, HEAD_DIM // 2, 1) * sin) * k_scale
        dm = dmat_ref[hh]
        qd = qdec_ref[:, lanes(hh)]
        kd = kdec_ref[:, lanes(hh)]
        cd = cdec_ref[:, lanes(hh)]
        gn = rng_ref[0][:, lanes(hh)]
        st = state_ref[hh]
        for c in range(n_chunks):
            qc = qh[rows(c)]
            kc = kh[rows(c)]
            vcb = vr[rows(c), lanes(hh)]
            s = lax.dot_general(qc.astype(BF16), kc.astype(BF16),
                                (((1,), (1,)), ((), ())), preferred_element_type=F32) * dm
            o = jnp.dot(s.astype(BF16), vcb, preferred_element_type=F32)
            o = o + jnp.dot((qc * qd).astype(BF16), st.astype(BF16), preferred_element_type=F32)
            kdt = (kc * kd).T.astype(BF16)
            st = st * cd + jnp.dot(kdt, vcb, preferred_element_type=F32)
            o = o * lax.rsqrt(jnp.mean(o * o, axis=-1, keepdims=True) + EPS) * gn
            ob = o * _silu(g[rows(c), lanes(hh)])
            ob_ref[rows(c), GROUP_W + hh * HEAD_DIM:GROUP_W + (hh + 1) * HEAD_DIM] = ob.astype(BF16)
        state_ref[hh] = st

    y = jnp.dot(ob_ref[...], wout_ref[0], preferred_element_type=F32)
    x1 = x + gtm_ref[0] * y

    ms2 = jnp.mean(x1 * x1, axis=-1, keepdims=True)
    h2 = x1 * lax.rsqrt(ms2 + EPS) * (gffn_ref[0] * (1.0 + scf_ref[0])) + shf_ref[0]
    h2b = h2.astype(BF16)

    if fuse_prev:
        unperm_sc[...] = _unpermute_matrix(lposn_ref[...], wn_ref[...])

    lg = lax.dot_general(wr_ref[0], h2b, (((1,), (1,)), ((), ())),
                         preferred_element_type=F32) + br_ref[0]
    e_io = lax.broadcasted_iota(I32, (N_EXPERTS, TM), 0)
    work = lg
    vals, ids = [], []
    for _ in range(TOP_K):
        m = jnp.max(work, axis=0, keepdims=True)
        ik = jnp.min(jnp.where(work == m, e_io, N_EXPERTS), axis=0, keepdims=True)
        vals.append(m)
        ids.append(ik)
        work = jnp.where(e_io == ik, -jnp.inf, work)
    exps = [jnp.exp(v - vals[0]) for v in vals]
    denom = exps[0] + exps[1] + exps[2] + exps[3]
    wts = [e / denom for e in exps]

    xl_ref[0] = jnp.dot(_permute_matrix(lpos_sc[...]), h2_sc[...], preferred_element_type=F32)

    sel = [e_io == ik for ik in ids]
    oh = jnp.where(sel[0] | sel[1] | sel[2] | sel[3], 1.0, 0.0)
    r_io = lax.broadcasted_iota(I32, (TM, TM), 0)
    c_io = lax.broadcasted_iota(I32, (TM, TM), 1)
    upper = jnp.where(r_io < c_io, 1.0, 0.0).astype(BF16)
    cum = jnp.dot(oh.astype(BF16), upper, preferred_element_type=F32)
    cnt_t = jnp.sum(oh, axis=1, keepdims=True)
    run = jnp.ceil(cnt_t * (1.0 / GROUP)) * GROUP
    run_b = jnp.broadcast_to(run, (N_EXPERTS, TM))
    er_io = lax.broadcasted_iota(I32, (N_EXPERTS, N_EXPERTS), 0)
    ec_io = lax.broadcasted_iota(I32, (N_EXPERTS, N_EXPERTS), 1)
    lower = jnp.where(ec_io < er_io, 1.0, 0.0).astype(BF16)
    lstart = jnp.dot(lower, run_b.astype(BF16), preferred_element_type=F32)
    pos = lstart + cum
    lpos = [jnp.sum(jnp.where(sk, pos, 0.0), axis=0, keepdims=True) for sk in sel]
    gbase = base_ref[...]
    row8 = lax.broadcasted_iota(I32, (8, TM), 0)

    def stack4(parts, fill):
        out = jnp.full((8, TM), fill, parts[0].dtype)
        for kk in range(TOP_K):
            out = jnp.where(row8 == kk, parts[kk], out)
        return out

    lpos8 = stack4([p.astype(I32) for p in lpos], 0)
    wgt8 = stack4(wts, 0.0)

    @pl.when(live)
    def _():
        x1_ref[0] = x1
        base_ref[...] = gbase + run_b
        cnt_ref[...] = (gbase + run_b)[:, :128]
        tab_ref[0, 0] = gbase[:, :128].astype(I32)
        tab_ref[0, 1] = run_b[:, :128].astype(I32)
        tab_ref[0, 2] = lstart[:, :128].astype(I32)
        lpos_ref[...] = lpos8
        wgt_ref[...] = wgt8
        h2_sc[...] = h2b
        lpos_sc[...] = lpos8


def _mixer_call(layer, x, mods, g_mix, g_ffn, w_in_b, ln_g, ln_b, w_sp, bs_full, rn_g,
                w_out_b, cos_t, sin_t, dmat, qdec, kdec, cdec, wr_t, br, prev=None):
    bsz, seq, d = x.shape
    n_tok = bsz * seq
    nj = seq // TM
    n_tiles = bsz * nj
    shm, scm, gtm, shf, scf = mods
    tile = lambda t: jnp.minimum(t, n_tiles - 1)
    vec = lambda: pl.BlockSpec((1, 1, d), lambda t: (tile(t) // nj, 0, 0))
    lvec = lambda w: pl.BlockSpec((1, 1, w), lambda t: (layer, 0, 0))
    full = lambda shp: pl.BlockSpec(shp, lambda t: (0,) * len(shp))
    tok = lambda shp: pl.BlockSpec(shp, lambda t: (0, tile(t)))
    in_specs = [
        pl.BlockSpec((1, TM, d), lambda t: (tile(t) // nj, tile(t) % nj, 0)),
        vec(), vec(), vec(), vec(), vec(),
        lvec(d), lvec(d),
        pl.BlockSpec((1, d, 6 * GROUP_W), lambda t: (layer, 0, 0)),
        lvec(GROUP_W), lvec(GROUP_W),
        pl.BlockSpec((1, N_HEADS, CHUNK, CHUNK), lambda t: (layer, 0, 0, 0)),
        pl.BlockSpec((1, CHUNK, GROUP_W), lambda t: (layer, 0, 0)),
        lvec(GROUP_W),
        pl.BlockSpec((1, d, d), lambda t: (layer, 0, 0)),
        pl.BlockSpec((TM, HEAD_DIM), lambda t: (tile(t) % nj, 0)),
        pl.BlockSpec((TM, HEAD_DIM), lambda t: (tile(t) % nj, 0)),
        full((N_HEADS, CHUNK, CHUNK)),
        full((CHUNK, GROUP_W)), full((CHUNK, GROUP_W)), full((1, GROUP_W)),
        pl.BlockSpec((1, N_EXPERTS, d), lambda t: (layer, 0, 0)),
        pl.BlockSpec((1, N_EXPERTS, 1), lambda t: (layer, 0, 0)),
    ]
    out_specs = [
        pl.BlockSpec((1, TM, d), lambda t: (tile(t) // nj, tile(t) % nj, 0)),
        pl.BlockSpec((1, LOCAL_ROWS, d), lambda t: (jnp.maximum(t - 1, 0), 0, 0)),
        tok((8, TM)), tok((8, TM)),
        pl.BlockSpec((1, 3, N_EXPERTS, 128), lambda t: (tile(t), 0, 0, 0)),
        pl.BlockSpec((N_EXPERTS, 128), lambda t: (0, 0)),
    ]
    out_shape = [
        jax.ShapeDtypeStruct((bsz, seq, d), F32),
        jax.ShapeDtypeStruct((n_tiles, LOCAL_ROWS, d), F32),
        jax.ShapeDtypeStruct((8, n_tok), I32),
        jax.ShapeDtypeStruct((8, n_tok), F32),
        jax.ShapeDtypeStruct((n_tiles, 3, N_EXPERTS, 128), I32),
        jax.ShapeDtypeStruct((N_EXPERTS, 128), F32),
    ]
    args = (x, shm, scm, gtm, shf, scf, g_mix, g_ffn, w_in_b, ln_g, ln_b, w_sp, bs_full, rn_g,
            w_out_b, cos_t, sin_t, dmat, qdec, kdec, cdec, wr_t, br)
    scratch = [
        pltpu.VMEM((N_HEADS, HEAD_DIM, HEAD_DIM), F32),
        pltpu.VMEM((N_EXPERTS, TM), F32),
        pltpu.VMEM((TM, d), BF16),
        pltpu.VMEM((TM, d), BF16),
        pltpu.VMEM((8, TM), I32),
    ]
    if prev is not None:
        yl, lpos_tok, w_tok, gtf = prev
        nxt = lambda t: jnp.minimum(t + 1, n_tiles - 1)
        in_specs = [
            pl.BlockSpec((1, LOCAL_ROWS, d), lambda t: (tile(t), 0, 0)),
            pl.BlockSpec((TM, TOP_K), lambda t: (tile(t), 0)),
            pl.BlockSpec((TM, TOP_K), lambda t: (tile(t), 0)),
            pl.BlockSpec((TM, TOP_K), lambda t: (nxt(t), 0)),
            pl.BlockSpec((TM, TOP_K), lambda t: (nxt(t), 0)),
            vec(),
        ] + in_specs
        args = (yl, lpos_tok, w_tok, lpos_tok, w_tok, gtf) + args
        scratch.append(pltpu.VMEM((TM, LOCAL_ROWS), BF16))
    return pl.pallas_call(
        functools.partial(_mixer_kernel, fuse_prev=prev is not None, nj=nj, n_tiles=n_tiles),
        grid=(n_tiles + 1,),
        in_specs=in_specs,
        out_specs=out_specs,
        out_shape=out_shape,
        scratch_shapes=scratch,
        compiler_params=pltpu.CompilerParams(
            dimension_semantics=("arbitrary",), vmem_limit_bytes=VMEM_LIMIT),
        name="mixer_router",
    )(*args)


def _ffn_kernel(be_ref, nused_ref, tab_ref, tabn_ref, w1_ref, b1_ref, w2_ref, b2_ref,
                xl_hbm, yl_hbm, xbuf, ybuf, zbuf, w1b, w2b, sem_in, sem_out, zsem):
    i = pl.program_id(0)
    par = i % 2
    n_used = nused_ref[0]
    tile_pieces = LOCAL_ROWS // GROUP
    pad_pieces = PAD_ROWS // GROUP
    n_tiles = yl_hbm.shape[0] // tile_pieces - 1
    d = xbuf.shape[-1]

    @pl.when((i < n_used) & ((i == 0) | (be_ref[i] != be_ref[jnp.maximum(i - 1, 0)])))
    def _():
        w1b[...] = w1_ref[0].astype(BF16)
        w2b[...] = w2_ref[0].astype(BF16)

    def zero_copy(piece0):
        return pltpu.make_async_copy(zbuf, yl_hbm.at[pl.ds(piece0, pad_pieces)], zsem)

    n_zero = n_tiles + tile_pieces // pad_pieces

    def zero_piece0(z):
        return jnp.where(z < n_tiles, z * tile_pieces + TOP_K * TM // GROUP,
                         n_tiles * tile_pieces + (z - n_tiles) * pad_pieces)

    @pl.when(i == 0)
    def _():
        zbuf[...] = jnp.zeros_like(zbuf)

        def body(z, c):
            zero_copy(zero_piece0(z)).start()
            return c
        lax.fori_loop(0, n_zero, body, 0)

    def start_fetch(tab, slot):
        def body(g, c):
            for u in range(COPY_UNROLL):
                p = g * COPY_UNROLL + u
                pltpu.make_async_copy(xl_hbm.at[tab[p]], xbuf.at[slot, p], sem_in.at[slot]).start()
            return c
        lax.fori_loop(0, PIECES // COPY_UNROLL, body, 0)

    def wait_fetch(slot):
        pltpu.make_async_copy(xl_hbm.at[pl.ds(0, PIECES)], xbuf.at[slot], sem_in.at[slot]).wait()

    def wait_store(slot):
        pltpu.make_async_copy(ybuf.at[slot], yl_hbm.at[pl.ds(0, PIECES)], sem_out.at[slot]).wait()

    @pl.when(i == 0)
    def _():
        start_fetch(tab_ref, 0)

    @pl.when(i + 1 < n_used)
    def _():
        start_fetch(tabn_ref, 1 - par)

    @pl.when(i < n_used)
    def _():
        wait_fetch(par)

        @pl.when(i >= 2)
        def _():
            wait_store(par)

        xb = xbuf[par].reshape(BM, d).astype(BF16)
        a = jnp.dot(xb, w1b[...], preferred_element_type=F32) + b1_ref[0]
        a_glu = jnp.minimum(a[:, :EXPERT_FF], SWIGLU_LIMIT)
        a_lin = jnp.clip(a[:, EXPERT_FF:], -SWIGLU_LIMIT, SWIGLU_LIMIT)
        act = a_glu * jax.nn.sigmoid(SWIGLU_ALPHA * a_glu) * (a_lin + 1.0)
        yb = jnp.dot(act.astype(BF16), w2b[...], preferred_element_type=F32) + b2_ref[0]
        ybuf[par] = yb.reshape(PIECES, GROUP, d)

        @pl.when(i == 0)
        def _():
            def body(z, c):
                zero_copy(0).wait()
                return c
            lax.fori_loop(0, n_zero, body, 0)

        def store(g, c):
            for u in range(COPY_UNROLL):
                p = g * COPY_UNROLL + u
                pltpu.make_async_copy(ybuf.at[par, p], yl_hbm.at[tab_ref[PIECES + p]],
                                      sem_out.at[par]).start()
            return c
        lax.fori_loop(0, PIECES // COPY_UNROLL, store, 0)

        @pl.when(i == n_used - 1)
        def _():
            wait_store(par)

            @pl.when(i >= 1)
            def _():
                wait_store(1 - par)


def _ffn_call(block_expert, n_used, piece_tab, xl, w1, b1, w2, b2):
    n_tiles, _, d = xl.shape
    n_blocks = piece_tab.shape[0] // TAB
    tile_pieces = LOCAL_ROWS // GROUP
    xl3 = xl.reshape(n_tiles * tile_pieces, GROUP, d)
    yl_shape = (n_tiles + 1, LOCAL_ROWS, d)
    grid_spec = pltpu.PrefetchScalarGridSpec(
        num_scalar_prefetch=2,
        grid=(n_blocks,),
        in_specs=[
            pl.BlockSpec((TAB,), lambda i, be, nu: (i,), memory_space=pltpu.SMEM),
            pl.BlockSpec((TAB,), lambda i, be, nu: (jnp.minimum(i + 1, n_blocks - 1),),
                         memory_space=pltpu.SMEM),
            pl.BlockSpec((1, d, 2 * EXPERT_FF), lambda i, be, nu: (be[i], 0, 0)),
            pl.BlockSpec((1, 1, 2 * EXPERT_FF), lambda i, be, nu: (be[i], 0, 0)),
            pl.BlockSpec((1, EXPERT_FF, d), lambda i, be, nu: (be[i], 0, 0)),
            pl.BlockSpec((1, 1, d), lambda i, be, nu: (be[i], 0, 0)),
            pl.BlockSpec(memory_space=pl.ANY),
        ],
        out_specs=pl.BlockSpec(memory_space=pl.ANY),
        scratch_shapes=[pltpu.VMEM((2, PIECES, GROUP, d), F32), pltpu.VMEM((2, PIECES, GROUP, d), F32),
                        pltpu.VMEM((PAD_ROWS // GROUP, GROUP, d), F32),
                        pltpu.VMEM((d, 2 * EXPERT_FF), BF16), pltpu.VMEM((EXPERT_FF, d), BF16),
                        pltpu.SemaphoreType.DMA((2,)), pltpu.SemaphoreType.DMA((2,)),
                        pltpu.SemaphoreType.DMA],
    )
    out = pl.pallas_call(
        _ffn_kernel,
        grid_spec=grid_spec,
        out_shape=jax.ShapeDtypeStruct(((n_tiles + 1) * tile_pieces, GROUP, d), F32),
        compiler_params=pltpu.CompilerParams(
            dimension_semantics=("arbitrary",), vmem_limit_bytes=VMEM_LIMIT),
        name="moe_ffn",
    )(block_expert, n_used, piece_tab, piece_tab, w1, b1, w2, b2, xl3)
    return out.reshape(yl_shape)


def _combine_kernel(yl_ref, lpos_ref, w_ref, x1_ref, gt_ref, gfin_ref, o_ref):
    acc = jnp.dot(_unpermute_matrix(lpos_ref[...], w_ref[...]), yl_ref[0].astype(BF16),
                  preferred_element_type=F32)
    out = x1_ref[...] + gt_ref[0] * acc
    o_ref[...] = out * lax.rsqrt(jnp.mean(out * out, axis=-1, keepdims=True) + EPS) * gfin_ref[...]


def _combine_call(yl, lpos_tok, w_tok, x1, gtf, g_final, seq):
    n_tok, d = x1.shape
    per_b = seq // TM
    return pl.pallas_call(
        _combine_kernel,
        grid=(n_tok // TM,),
        in_specs=[
            pl.BlockSpec((1, LOCAL_ROWS, d), lambda i: (i, 0, 0)),
            pl.BlockSpec((TM, TOP_K), lambda i: (i, 0)),
            pl.BlockSpec((TM, TOP_K), lambda i: (i, 0)),
            pl.BlockSpec((TM, d), lambda i: (i, 0)),
            pl.BlockSpec((1, 1, d), lambda i: (i // per_b, 0, 0)),
            pl.BlockSpec((1, d), lambda i: (0, 0)),
        ],
        out_specs=pl.BlockSpec((TM, d), lambda i: (i, 0)),
        out_shape=jax.ShapeDtypeStruct((n_tok, d), F32),
        compiler_params=pltpu.CompilerParams(
            dimension_semantics=("arbitrary",), vmem_limit_bytes=VMEM_LIMIT),
        name="moe_combine",
    )(yl, lpos_tok, w_tok, x1, gtf, g_final)


def _tables(seq):
    half = HEAD_DIM // 2
    inv_freq = ROPE_BASE ** (-jnp.arange(half, dtype=F32) / half)
    ang = jnp.arange(seq, dtype=F32)[:, None] * inv_freq[None, :]
    cos = jnp.cos(ang)
    sin = jnp.sin(ang)
    cos_t = jnp.concatenate([cos, cos], axis=-1)
    sin_t = jnp.concatenate([-sin, sin], axis=-1)
    gamma = 1.0 - jnp.exp2(-5.0 - jnp.arange(N_HEADS, dtype=F32))
    log_g = jnp.log(gamma)
    idx = jnp.arange(CHUNK, dtype=F32)
    diff = idx[:, None] - idx[None, :]
    dmat = jnp.where(diff[None] >= 0,
                     jnp.exp(log_g[:, None, None] * jnp.maximum(diff, 0.0)[None]), 0.0)
    k_decay = jnp.exp(log_g[:, None] * (CHUNK - 1.0 - idx)[None, :])
    q_decay = jnp.exp(log_g[:, None] * (idx + 1.0)[None, :])
    chunk_decay = jnp.exp(log_g * CHUNK)
    widen = lambda t: jnp.repeat(t.T[:, :, None], HEAD_DIM, axis=2).reshape(CHUNK, GROUP_W)
    qdec = widen(q_decay)
    kdec = widen(k_decay)
    cdec = jnp.repeat(chunk_decay[:, None], HEAD_DIM, axis=1).reshape(1, GROUP_W)
    return cos_t, sin_t, dmat, qdec, kdec, cdec


def kernel(x, c, w_ada, b_ada, g_mix, w_in, gmlp_ln_g, gmlp_ln_b, w_spatial, b_spatial,
           ret_norm_g, w_out, g_ffn, w_router, b_router, w1, b1, w2, b2, g_final):
    bsz, seq, d = x.shape
    depth = w_in.shape[0]
    n_tok = bsz * seq
    n_tiles = n_tok // TM
    n_slots = n_tok * TOP_K + n_tiles * N_EXPERTS * GROUP + N_EXPERTS * BM
    assert n_slots % BM == 0
    n_blocks = n_slots // BM

    mod = _ada_call(c, w_ada, b_ada)
    cos_t, sin_t, dmat, qdec, kdec, cdec = _tables(seq)

    w_in_b = w_in.astype(BF16)
    w_out_b = w_out.astype(BF16)
    wr_t = jnp.swapaxes(w_router, 1, 2).astype(BF16)
    br = b_router.reshape(depth, N_EXPERTS, 1)
    w1_r = w1.reshape(depth * N_EXPERTS, d, 2 * EXPERT_FF)
    w2_r = w2.reshape(depth * N_EXPERTS, EXPERT_FF, d)
    b1_r = b1.reshape(depth * N_EXPERTS, 1, 2 * EXPERT_FF)
    b2_r = b2.reshape(depth * N_EXPERTS, 1, d)
    bs_full = jnp.repeat(jnp.swapaxes(b_spatial, 1, 2)[:, :, :, None], HEAD_DIM, axis=3)
    bs_full = bs_full.reshape(depth, CHUNK, GROUP_W)
    row3 = lambda a: a.reshape(depth, 1, a.shape[-1])
    gfin = g_final.reshape(1, d)
    e_ids = jnp.arange(N_EXPERTS, dtype=I32)
    blk = jnp.arange(n_blocks, dtype=I32)

    piece_off = GROUP * jnp.arange(PIECES, dtype=I32)
    tile_row0 = LOCAL_ROWS * jnp.arange(n_tiles, dtype=I32)
    zero_row = LOCAL_ROWS - GROUP
    spare_row0 = n_tiles * LOCAL_ROWS
    prev = None
    for l in range(depth):
        mods = [m.reshape(bsz, 1, d) for m in jnp.split(mod[l], N_MOD, axis=-1)]
        shm, scm, gtm, shf, scf, gtf = mods
        x1, xl, lpos8, wgt8, tab3, cnt = _mixer_call(
            l, x, (shm, scm, gtm, shf, scf), row3(g_mix), row3(g_ffn), w_in_b,
            row3(gmlp_ln_g), row3(gmlp_ln_b), w_spatial, bs_full, row3(ret_norm_g), w_out_b,
            cos_t, sin_t, dmat, qdec, kdec, cdec, wr_t, br, prev=prev)

        cntp = cnt[:, 0].astype(I32)
        padded = ((cntp + BM - 1) // BM) * BM
        pend = jnp.sum(jnp.where(e_ids[None, :] <= e_ids[:, None], padded[None, :], 0), axis=1)
        pstart = pend - padded
        n_used = pend[-1] // BM
        block_e = jnp.minimum(
            jnp.sum((pend[None, :] <= (blk * BM)[:, None]).astype(I32), axis=1), N_EXPERTS - 1)
        gbase, runs, lstart = tab3[:, 0, :, 0], tab3[:, 1, :, 0], tab3[:, 2, :, 0]
        r = (blk * BM - pstart[block_e])[:, None] + piece_off[None, :]
        gb = gbase.T[block_e]
        rn = runs.T[block_e]
        delta = tile_row0[None, :] + lstart.T[block_e] - gb
        in_run = (gb[:, None, :] <= r[:, :, None]) & (r[:, :, None] < (gb + rn)[:, None, :])
        row = r + jnp.sum(jnp.where(in_run, delta[:, None, :], 0), axis=-1)
        live = (r < cntp[block_e][:, None]) & (blk < n_used)[:, None]
        src = jnp.where(live, row, zero_row)
        dst = jnp.where(live, row, spare_row0 + (blk % 2)[:, None] * BM + piece_off[None, :])
        piece_tab = jnp.concatenate(
            [src // GROUP, dst // GROUP, jnp.zeros((n_blocks, TAB - 2 * PIECES), I32)],
            axis=1).reshape(-1)

        yl = _ffn_call(block_e + l * N_EXPERTS, n_used.reshape(1), piece_tab, xl,
                       w1_r, b1_r, w2_r, b2_r)
        x = x1
        prev = (yl, lpos8[:TOP_K].T, wgt8[:TOP_K].T, gtf)
    out = _combine_call(prev[0], prev[1], prev[2], x.reshape(n_tok, d), prev[3], gfin, seq)
    return out.reshape(bsz, seq, d)
```

```python
import functools

import jax
import jax.numpy as jnp
from jax import lax
from jax.experimental import pallas as pl
from jax.experimental.pallas import tpu as pltpu

F32 = jnp.float32
BF16 = jnp.bfloat16
I32 = jnp.int32

D_MODEL = 1024
CHUNK = 128
N_HEADS = 4
HEAD_DIM = 128
GROUP_W = N_HEADS * HEAD_DIM
N_EXPERTS = 32
TOP_K = 4
EXPERT_FF = 1024
N_MOD = 6
EPS = 1e-5
ROPE_BASE = 10000.0
SWIGLU_ALPHA = 1.702
SWIGLU_LIMIT = 7.0

TM = 256
BM = 512
GROUP = 8
PAD_ROWS = N_EXPERTS * GROUP
LOCAL_ROWS = TOP_K * TM + PAD_ROWS
PIECES = BM // GROUP
COPY_UNROLL = 8
TAB = 1024
VMEM_LIMIT = 56 * 1024 * 1024


def _gelu_tanh(x):
    return x * (0.5 * (1.0 + jnp.tanh(0.7978845608028654 * (x + 0.044715 * (x * x * x)))))


def _silu(x):
    return x * jax.nn.sigmoid(x)


def _ada_kernel(c_ref, w_ref, b_ref, o_ref):
    ca = _silu(c_ref[...]).astype(BF16)
    o_ref[0] = jnp.dot(ca, w_ref[0].astype(BF16), preferred_element_type=F32) + b_ref[0]


def _ada_call(c, w_ada, b_ada):
    depth, d, _ = w_ada.shape
    bsz = c.shape[0]
    return pl.pallas_call(
        _ada_kernel,
        grid=(depth, N_MOD),
        in_specs=[
            pl.BlockSpec((bsz, d), lambda l, n: (0, 0)),
            pl.BlockSpec((1, d, d), lambda l, n: (l, 0, n)),
            pl.BlockSpec((1, 1, d), lambda l, n: (l, 0, n)),
        ],
        out_specs=pl.BlockSpec((1, bsz, d), lambda l, n: (l, 0, n)),
        out_shape=jax.ShapeDtypeStruct((depth, bsz, N_MOD * d), F32),
        name="adaln_mod",
    )(c, w_ada, b_ada.reshape(depth, 1, N_MOD * d))


def _unpermute_matrix(lpos_tok, w_tok):
    l_io = lax.broadcasted_iota(I32, (TM, LOCAL_ROWS), 1).astype(jnp.int16)
    lp = lpos_tok.astype(jnp.int16)
    wb = w_tok.astype(BF16)
    unperm = jnp.zeros((TM, LOCAL_ROWS), BF16)
    for kk in range(TOP_K):
        unperm = jnp.where(l_io == lp[:, kk:kk + 1], wb[:, kk:kk + 1], unperm)
    return unperm


def _permute_matrix(lpos_rows):
    lr_io = lax.broadcasted_iota(I32, (LOCAL_ROWS, TM), 0).astype(jnp.int16)
    one = jnp.ones((LOCAL_ROWS, TM), BF16)
    perm = jnp.zeros((LOCAL_ROWS, TM), BF16)
    for kk in range(TOP_K):
        perm = jnp.where(lr_io == lpos_rows[kk:kk + 1].astype(jnp.int16), one, perm)
    return perm


def _mixer_kernel(*refs, fuse_prev, nj, n_tiles):
    if fuse_prev:
        (ylp_ref, lposp_ref, wp_ref, lposn_ref, wn_ref, gtp_ref), refs = refs[:6], refs[6:]
        unperm_sc, refs = refs[-1], refs[:-1]
    (x_ref, shm_ref, scm_ref, gtm_ref, shf_ref, scf_ref, gmix_ref, gffn_ref,
     win_ref, lng_ref, lnb_ref, ws_ref, bs_ref, rng_ref, wout_ref,
     cos_ref, sin_ref, dmat_ref, qdec_ref, kdec_ref, cdec_ref, wr_ref, br_ref,
     x1_ref, xl_ref, lpos_ref, wgt_ref, tab_ref, cnt_ref,
     state_ref, base_ref, ob_ref, h2_sc, lpos_sc) = refs
    t = pl.program_id(0)
    live = t < n_tiles
    j = jnp.minimum(t, n_tiles - 1) % nj
    n_chunks = TM // CHUNK

    @pl.when(j == 0)
    def _():
        state_ref[...] = jnp.zeros_like(state_ref)

    @pl.when(t == 0)
    def _():
        base_ref[...] = jnp.zeros_like(base_ref)
        h2_sc[...] = jnp.zeros_like(h2_sc)
        lpos_sc[...] = jnp.zeros_like(lpos_sc)
        if fuse_prev:
            unperm_sc[...] = _unpermute_matrix(lposp_ref[...], wp_ref[...])

    x = x_ref[0]
    if fuse_prev:
        x = x + gtp_ref[0] * jnp.dot(unperm_sc[...], ylp_ref[0].astype(BF16),
                                     preferred_element_type=F32)

    ms = jnp.mean(x * x, axis=-1, keepdims=True)
    h = x * lax.rsqrt(ms + EPS) * (gmix_ref[0] * (1.0 + scm_ref[0])) + shm_ref[0]
    hb = h.astype(BF16)

    def proj(k):
        return jnp.dot(hb, win_ref[0, :, k * GROUP_W:(k + 1) * GROUP_W],
                       preferred_element_type=F32)

    def rows(c):
        return slice(c * CHUNK, (c + 1) * CHUNK)

    def lanes(hh):
        return slice(hh * HEAD_DIM, (hh + 1) * HEAD_DIM)

    gu = _gelu_tanh(proj(0))
    gv = _gelu_tanh(proj(1))
    mu = jnp.mean(gv, axis=-1, keepdims=True)
    vc = gv - mu
    var = jnp.mean(vc * vc, axis=-1, keepdims=True)
    vn = (vc * lax.rsqrt(var + EPS) * lng_ref[0] + lnb_ref[0]).astype(BF16)
    t_io = lax.broadcasted_iota(I32, (CHUNK, CHUNK), 0)
    s_io = lax.broadcasted_iota(I32, (CHUNK, CHUNK), 1)
    causal = t_io >= s_io
    bias = bs_ref[0]
    for hh in range(N_HEADS):
        wm = jnp.where(causal, ws_ref[0, hh], 0.0).astype(BF16)
        for c in range(n_chunks):
            mixed = jnp.dot(wm, vn[rows(c), lanes(hh)], preferred_element_type=F32)
            oa = gu[rows(c), lanes(hh)] * (mixed + bias[:, lanes(hh)])
            ob_ref[rows(c), lanes(hh)] = oa.astype(BF16)

    q = proj(2)
    k = proj(3)
    vr = proj(4).astype(BF16)
    g = proj(5)
    cos = cos_ref[...]
    sin = sin_ref[...]
    k_scale = HEAD_DIM ** -0.5
    for hh in range(N_HEADS):
        qh = q[:, lanes(hh)]
        kh = k[:, lanes(hh)]
        qh = qh * cos + pltpu.roll(qh, HEAD_DIM // 2, 1) * sin
        kh = (kh * cos + pltpu.roll(kh, HEAD_DIM // 2, 1) * sin) * k_scale
        dm = dmat_ref[hh]
        qd = qdec_ref[:, lanes(hh)]
        kd = kdec_ref[:, lanes(hh)]
        cd = cdec_ref[:, lanes(hh)]
        gn = rng_ref[0][:, lanes(hh)]
        st = state_ref[hh]
        for c in range(n_chunks):
            qc = qh[rows(c)]
            kc = kh[rows(c)]
            vcb = vr[rows(c), lanes(hh)]
            s = lax.dot_general(qc.astype(BF16), kc.astype(BF16),
                                (((1,), (1,)), ((), ())), preferred_element_type=F32) * dm
            o = jnp.dot(s.astype(BF16), vcb, preferred_element_type=F32)
            o = o + jnp.dot((qc * qd).astype(BF16), st.astype(BF16), preferred_element_type=F32)
            kdt = (kc * kd).T.astype(BF16)
            st = st * cd + jnp.dot(kdt, vcb, preferred_element_type=F32)
            o = o * lax.rsqrt(jnp.mean(o * o, axis=-1, keepdims=True) + EPS) * gn
            ob = o * _silu(g[rows(c), lanes(hh)])
            ob_ref[rows(c), GROUP_W + hh * HEAD_DIM:GROUP_W + (hh + 1) * HEAD_DIM] = ob.astype(BF16)
        state_ref[hh] = st

    y = jnp.dot(ob_ref[...], wout_ref[0], preferred_element_type=F32)
    x1 = x + gtm_ref[0] * y

    ms2 = jnp.mean(x1 * x1, axis=-1, keepdims=True)
    h2 = x1 * lax.rsqrt(ms2 + EPS) * (gffn_ref[0] * (1.0 + scf_ref[0])) + shf_ref[0]
    h2b = h2.astype(BF16)

    if fuse_prev:
        unperm_sc[...] = _unpermute_matrix(lposn_ref[...], wn_ref[...])

    lg = lax.dot_general(wr_ref[0], h2b, (((1,), (1,)), ((), ())),
                         preferred_element_type=F32) + br_ref[0]
    e_io = lax.broadcasted_iota(I32, (N_EXPERTS, TM), 0)
    work = lg
    vals, ids = [], []
    for _ in range(TOP_K):
        m = jnp.max(work, axis=0, keepdims=True)
        ik = jnp.min(jnp.where(work == m, e_io, N_EXPERTS), axis=0, keepdims=True)
        vals.append(m)
        ids.append(ik)
        work = jnp.where(e_io == ik, -jnp.inf, work)
    exps = [jnp.exp(v - vals[0]) for v in vals]
    denom = exps[0] + exps[1] + exps[2] + exps[3]
    wts = [e / denom for e in exps]

    xl_ref[0] = jnp.dot(_permute_matrix(lpos_sc[...]), h2_sc[...], preferred_element_type=F32)

    sel = [e_io == ik for ik in ids]
    oh = jnp.where(sel[0] | sel[1] | sel[2] | sel[3], 1.0, 0.0)
    r_io = lax.broadcasted_iota(I32, (TM, TM), 0)
    c_io = lax.broadcasted_iota(I32, (TM, TM), 1)
    upper = jnp.where(r_io < c_io, 1.0, 0.0).astype(BF16)
    cum = jnp.dot(oh.astype(BF16), upper, preferred_element_type=F32)
    cnt_t = jnp.sum(oh, axis=1, keepdims=True)
    run = jnp.ceil(cnt_t * (1.0 / GROUP)) * GROUP
    run_b = jnp.broadcast_to(run, (N_EXPERTS, TM))
    er_io = lax.broadcasted_iota(I32, (N_EXPERTS, N_EXPERTS), 0)
    ec_io = lax.broadcasted_iota(I32, (N_EXPERTS, N_EXPERTS), 1)
    lower = jnp.where(ec_io < er_io, 1.0, 0.0).astype(BF16)
    lstart = jnp.dot(lower, run_b.astype(BF16), preferred_element_type=F32)
    pos = lstart + cum
    lpos = [jnp.sum(jnp.where(sk, pos, 0.0), axis=0, keepdims=True) for sk in sel]
    gbase = base_ref[...]
    row8 = lax.broadcasted_iota(I32, (8, TM), 0)

    def stack4(parts, fill):
        out = jnp.full((8, TM), fill, parts[0].dtype)
        for kk in range(TOP_K):
            out = jnp.where(row8 == kk, parts[kk], out)
        return out

    lpos8 = stack4([p.astype(I32) for p in lpos], 0)
    wgt8 = stack4(wts, 0.0)

    @pl.when(live)
    def _():
        x1_ref[0] = x1
        base_ref[...] = gbase + run_b
        cnt_ref[...] = (gbase + run_b)[:, :128]
        tab_ref[0, 0] = gbase[:, :128].astype(I32)
        tab_ref[0, 1] = run_b[:, :128].astype(I32)
        tab_ref[0, 2] = lstart[:, :128].astype(I32)
        lpos_ref[...] = lpos8
        wgt_ref[...] = wgt8
        h2_sc[...] = h2b
        lpos_sc[...] = lpos8


def _mixer_call(layer, x, mods, g_mix, g_ffn, w_in_b, ln_g, ln_b, w_sp, bs_full, rn_g,
                w_out_b, cos_t, sin_t, dmat, qdec, kdec, cdec, wr_t, br, prev=None):
    bsz, seq, d = x.shape
    n_tok = bsz * seq
    nj = seq // TM
    n_tiles = bsz * nj
    shm, scm, gtm, shf, scf = mods
    tile = lambda t: jnp.minimum(t, n_tiles - 1)
    vec = lambda: pl.BlockSpec((1, 1, d), lambda t: (tile(t) // nj, 0, 0))
    lvec = lambda w: pl.BlockSpec((1, 1, w), lambda t: (layer, 0, 0))
    full = lambda shp: pl.BlockSpec(shp, lambda t: (0,) * len(shp))
    tok = lambda shp: pl.BlockSpec(shp, lambda t: (0, tile(t)))
    in_specs = [
        pl.BlockSpec((1, TM, d), lambda t: (tile(t) // nj, tile(t) % nj, 0)),
        vec(), vec(), vec(), vec(), vec(),
        lvec(d), lvec(d),
        pl.BlockSpec((1, d, 6 * GROUP_W), lambda t: (layer, 0, 0)),
        lvec(GROUP_W), lvec(GROUP_W),
        pl.BlockSpec((1, N_HEADS, CHUNK, CHUNK), lambda t: (layer, 0, 0, 0)),
        pl.BlockSpec((1, CHUNK, GROUP_W), lambda t: (layer, 0, 0)),
        lvec(GROUP_W),
        pl.BlockSpec((1, d, d), lambda t: (layer, 0, 0)),
        pl.BlockSpec((TM, HEAD_DIM), lambda t: (tile(t) % nj, 0)),
        pl.BlockSpec((TM, HEAD_DIM), lambda t: (tile(t) % nj, 0)),
        full((N_HEADS, CHUNK, CHUNK)),
        full((CHUNK, GROUP_W)), full((CHUNK, GROUP_W)), full((1, GROUP_W)),
        pl.BlockSpec((1, N_EXPERTS, d), lambda t: (layer, 0, 0)),
        pl.BlockSpec((1, N_EXPERTS, 1), lambda t: (layer, 0, 0)),
    ]
    out_specs = [
        pl.BlockSpec((1, TM, d), lambda t: (tile(t) // nj, tile(t) % nj, 0)),
        pl.BlockSpec((1, LOCAL_ROWS, d), lambda t: (jnp.maximum(t - 1, 0), 0, 0)),
        tok((8, TM)), tok((8, TM)),
        pl.BlockSpec((1, 3, N_EXPERTS, 128), lambda t: (tile(t), 0, 0, 0)),
        pl.BlockSpec((N_EXPERTS, 128), lambda t: (0, 0)),
    ]
    out_shape = [
        jax.ShapeDtypeStruct((bsz, seq, d), F32),
        jax.ShapeDtypeStruct((n_tiles, LOCAL_ROWS, d), F32),
        jax.ShapeDtypeStruct((8, n_tok), I32),
        jax.ShapeDtypeStruct((8, n_tok), F32),
        jax.ShapeDtypeStruct((n_tiles, 3, N_EXPERTS, 128), I32),
        jax.ShapeDtypeStruct((N_EXPERTS, 128), F32),
    ]
    args = (x, shm, scm, gtm, shf, scf, g_mix, g_ffn, w_in_b, ln_g, ln_b, w_sp, bs_full, rn_g,
            w_out_b, cos_t, sin_t, dmat, qdec, kdec, cdec, wr_t, br)
    scratch = [
        pltpu.VMEM((N_HEADS, HEAD_DIM, HEAD_DIM), F32),
        pltpu.VMEM((N_EXPERTS, TM), F32),
        pltpu.VMEM((TM, d), BF16),
        pltpu.VMEM((TM, d), BF16),
        pltpu.VMEM((8, TM), I32),
    ]
    if prev is not None:
        yl, lpos_tok, w_tok, gtf = prev
        nxt = lambda t: jnp.minimum(t + 1, n_tiles - 1)
        in_specs = [
            pl.BlockSpec((1, LOCAL_ROWS, d), lambda t: (tile(t), 0, 0)),
            pl.BlockSpec((TM, TOP_K), lambda t: (tile(t), 0)),
            pl.BlockSpec((TM, TOP_K), lambda t: (tile(t), 0)),
            pl.BlockSpec((TM, TOP_K), lambda t: (nxt(t), 0)),
            pl.BlockSpec((TM, TOP_K), lambda t: (nxt(t), 0)),
            vec(),
        ] + in_specs
        args = (yl, lpos_tok, w_tok, lpos_tok, w_tok, gtf) + args
        scratch.append(pltpu.VMEM((TM, LOCAL_ROWS), BF16))
    return pl.pallas_call(
        functools.partial(_mixer_kernel, fuse_prev=prev is not None, nj=nj, n_tiles=n_tiles),
        grid=(n_tiles + 1,),
        in_specs=in_specs,
        out_specs=out_specs,
        out_shape=out_shape,
        scratch_shapes=scratch,
        compiler_params=pltpu.CompilerParams(
            dimension_semantics=("arbitrary",), vmem_limit_bytes=VMEM_LIMIT),
        name="mixer_router",
    )(*args)


def _ffn_kernel(be_ref, nused_ref, tab_ref, tabn_ref, w1_ref, b1_ref, w2_ref, b2_ref,
                xl_hbm, yl_hbm, xbuf, ybuf, zbuf, w1b, w2b, sem_in, sem_out, zsem):
    i = pl.program_id(0)
    par = i % 2
    n_used = nused_ref[0]
    tile_pieces = LOCAL_ROWS // GROUP
    pad_pieces = PAD_ROWS // GROUP
    n_tiles = yl_hbm.shape[0] // tile_pieces - 1
    d = xbuf.shape[-1]

    @pl.when((i < n_used) & ((i == 0) | (be_ref[i] != be_ref[jnp.maximum(i - 1, 0)])))
    def _():
        w1b[...] = w1_ref[0].astype(BF16)
        w2b[...] = w2_ref[0].astype(BF16)

    def zero_copy(piece0):
        return pltpu.make_async_copy(zbuf, yl_hbm.at[pl.ds(piece0, pad_pieces)], zsem)

    n_zero = n_tiles + tile_pieces // pad_pieces

    def zero_piece0(z):
        return jnp.where(z < n_tiles, z * tile_pieces + TOP_K * TM // GROUP,
                         n_tiles * tile_pieces + (z - n_tiles) * pad_pieces)

    @pl.when(i == 0)
    def _():
        zbuf[...] = jnp.zeros_like(zbuf)

        def body(z, c):
            zero_copy(zero_piece0(z)).start()
            return c
        lax.fori_loop(0, n_zero, body, 0)

    def start_fetch(tab, slot):
        def body(g, c):
            for u in range(COPY_UNROLL):
                p = g * COPY_UNROLL + u
                pltpu.make_async_copy(xl_hbm.at[tab[p]], xbuf.at[slot, p], sem_in.at[slot]).start()
            return c
        lax.fori_loop(0, PIECES // COPY_UNROLL, body, 0)

    def wait_fetch(slot):
        pltpu.make_async_copy(xl_hbm.at[pl.ds(0, PIECES)], xbuf.at[slot], sem_in.at[slot]).wait()

    def wait_store(slot):
        pltpu.make_async_copy(ybuf.at[slot], yl_hbm.at[pl.ds(0, PIECES)], sem_out.at[slot]).wait()

    @pl.when(i == 0)
    def _():
        start_fetch(tab_ref, 0)

    @pl.when(i + 1 < n_used)
    def _():
        start_fetch(tabn_ref, 1 - par)

    @pl.when(i < n_used)
    def _():
        wait_fetch(par)

        @pl.when(i >= 2)
        def _():
            wait_store(par)

        xb = xbuf[par].reshape(BM, d).astype(BF16)
        a = jnp.dot(xb, w1b[...], preferred_element_type=F32) + b1_ref[0]
        a_glu = jnp.minimum(a[:, :EXPERT_FF], SWIGLU_LIMIT)
        a_lin = jnp.clip(a[:, EXPERT_FF:], -SWIGLU_LIMIT, SWIGLU_LIMIT)
        act = a_glu * jax.nn.sigmoid(SWIGLU_ALPHA * a_glu) * (a_lin + 1.0)
        yb = jnp.dot(act.astype(BF16), w2b[...], preferred_element_type=F32) + b2_ref[0]
        ybuf[par] = yb.reshape(PIECES, GROUP, d)

        @pl.when(i == 0)
        def _():
            def body(z, c):
                zero_copy(0).wait()
                return c
            lax.fori_loop(0, n_zero, body, 0)

        def store(g, c):
            for u in range(COPY_UNROLL):
                p = g * COPY_UNROLL + u
                pltpu.make_async_copy(ybuf.at[par, p], yl_hbm.at[tab_ref[PIECES + p]],
                                      sem_out.at[par]).start()
            return c
        lax.fori_loop(0, PIECES // COPY_UNROLL, store, 0)

        @pl.when(i == n_used - 1)
        def _():
            wait_store(par)

            @pl.when(i >= 1)
            def _():
                wait_store(1 - par)


def _ffn_call(block_expert, n_used, piece_tab, xl, w1, b1, w2, b2):
    n_tiles, _, d = xl.shape
    n_blocks = piece_tab.shape[0] // TAB
    tile_pieces = LOCAL_ROWS // GROUP
    xl3 = xl.reshape(n_tiles * tile_pieces, GROUP, d)
    yl_shape = (n_tiles + 1, LOCAL_ROWS, d)
    grid_spec = pltpu.PrefetchScalarGridSpec(
        num_scalar_prefetch=2,
        grid=(n_blocks,),
        in_specs=[
            pl.BlockSpec((TAB,), lambda i, be, nu: (i,), memory_space=pltpu.SMEM),
            pl.BlockSpec((TAB,), lambda i, be, nu: (jnp.minimum(i + 1, n_blocks - 1),),
                         memory_space=pltpu.SMEM),
            pl.BlockSpec((1, d, 2 * EXPERT_FF), lambda i, be, nu: (be[i], 0, 0)),
            pl.BlockSpec((1, 1, 2 * EXPERT_FF), lambda i, be, nu: (be[i], 0, 0)),
            pl.BlockSpec((1, EXPERT_FF, d), lambda i, be, nu: (be[i], 0, 0)),
            pl.BlockSpec((1, 1, d), lambda i, be, nu: (be[i], 0, 0)),
            pl.BlockSpec(memory_space=pl.ANY),
        ],
        out_specs=pl.BlockSpec(memory_space=pl.ANY),
        scratch_shapes=[pltpu.VMEM((2, PIECES, GROUP, d), F32), pltpu.VMEM((2, PIECES, GROUP, d), F32),
                        pltpu.VMEM((PAD_ROWS // GROUP, GROUP, d), F32),
                        pltpu.VMEM((d, 2 * EXPERT_FF), BF16), pltpu.VMEM((EXPERT_FF, d), BF16),
                        pltpu.SemaphoreType.DMA((2,)), pltpu.SemaphoreType.DMA((2,)),
                        pltpu.SemaphoreType.DMA],
    )
    out = pl.pallas_call(
        _ffn_kernel,
        grid_spec=grid_spec,
        out_shape=jax.ShapeDtypeStruct(((n_tiles + 1) * tile_pieces, GROUP, d), F32),
        compiler_params=pltpu.CompilerParams(
            dimension_semantics=("arbitrary",), vmem_limit_bytes=VMEM_LIMIT),
        name="moe_ffn",
    )(block_expert, n_used, piece_tab, piece_tab, w1, b1, w2, b2, xl3)
    return out.reshape(yl_shape)


def _combine_kernel(yl_ref, lpos_ref, w_ref, x1_ref, gt_ref, gfin_ref, o_ref):
    acc = jnp.dot(_unpermute_matrix(lpos_ref[...], w_ref[...]), yl_ref[0].astype(BF16),
                  preferred_element_type=F32)
    out = x1_ref[...] + gt_ref[0] * acc
    o_ref[...] = out * lax.rsqrt(jnp.mean(out * out, axis=-1, keepdims=True) + EPS) * gfin_ref[...]


def _combine_call(yl, lpos_tok, w_tok, x1, gtf, g_final, seq):
    n_tok, d = x1.shape
    per_b = seq // TM
    return pl.pallas_call(
        _combine_kernel,
        grid=(n_tok // TM,),
        in_specs=[
            pl.BlockSpec((1, LOCAL_ROWS, d), lambda i: (i, 0, 0)),
            pl.BlockSpec((TM, TOP_K), lambda i: (i, 0)),
            pl.BlockSpec((TM, TOP_K), lambda i: (i, 0)),
            pl.BlockSpec((TM, d), lambda i: (i, 0)),
            pl.BlockSpec((1, 1, d), lambda i: (i // per_b, 0, 0)),
            pl.BlockSpec((1, d), lambda i: (0, 0)),
        ],
        out_specs=pl.BlockSpec((TM, d), lambda i: (i, 0)),
        out_shape=jax.ShapeDtypeStruct((n_tok, d), F32),
        compiler_params=pltpu.CompilerParams(
            dimension_semantics=("arbitrary",), vmem_limit_bytes=VMEM_LIMIT),
        name="moe_combine",
    )(yl, lpos_tok, w_tok, x1, gtf, g_final)


def _tables(seq):
    half = HEAD_DIM // 2
    inv_freq = ROPE_BASE ** (-jnp.arange(half, dtype=F32) / half)
    ang = jnp.arange(seq, dtype=F32)[:, None] * inv_freq[None, :]
    cos = jnp.cos(ang)
    sin = jnp.sin(ang)
    cos_t = jnp.concatenate([cos, cos], axis=-1)
    sin_t = jnp.concatenate([-sin, sin], axis=-1)
    gamma = 1.0 - jnp.exp2(-5.0 - jnp.arange(N_HEADS, dtype=F32))
    log_g = jnp.log(gamma)
    idx = jnp.arange(CHUNK, dtype=F32)
    diff = idx[:, None] - idx[None, :]
    dmat = jnp.where(diff[None] >= 0,
                     jnp.exp(log_g[:, None, None] * jnp.maximum(diff, 0.0)[None]), 0.0)
    k_decay = jnp.exp(log_g[:, None] * (CHUNK - 1.0 - idx)[None, :])
    q_decay = jnp.exp(log_g[:, None] * (idx + 1.0)[None, :])
    chunk_decay = jnp.exp(log_g * CHUNK)
    widen = lambda t: jnp.repeat(t.T[:, :, None], HEAD_DIM, axis=2).reshape(CHUNK, GROUP_W)
    qdec = widen(q_decay)
    kdec = widen(k_decay)
    cdec = jnp.repeat(chunk_decay[:, None], HEAD_DIM, axis=1).reshape(1, GROUP_W)
    return cos_t, sin_t, dmat, qdec, kdec, cdec


def kernel(x, c, w_ada, b_ada, g_mix, w_in, gmlp_ln_g, gmlp_ln_b, w_spatial, b_spatial,
           ret_norm_g, w_out, g_ffn, w_router, b_router, w1, b1, w2, b2, g_final):
    bsz, seq, d = x.shape
    depth = w_in.shape[0]
    n_tok = bsz * seq
    n_tiles = n_tok // TM
    n_slots = n_tok * TOP_K + n_tiles * N_EXPERTS * GROUP + N_EXPERTS * BM
    assert n_slots % BM == 0
    n_blocks = n_slots // BM

    mod = _ada_call(c, w_ada, b_ada)
    cos_t, sin_t, dmat, qdec, kdec, cdec = _tables(seq)

    w_in_b = w_in.astype(BF16)
    w_out_b = w_out.astype(BF16)
    wr_t = jnp.swapaxes(w_router, 1, 2).astype(BF16)
    br = b_router.reshape(depth, N_EXPERTS, 1)
    w1_r = w1.reshape(depth * N_EXPERTS, d, 2 * EXPERT_FF)
    w2_r = w2.reshape(depth * N_EXPERTS, EXPERT_FF, d)
    b1_r = b1.reshape(depth * N_EXPERTS, 1, 2 * EXPERT_FF)
    b2_r = b2.reshape(depth * N_EXPERTS, 1, d)
    bs_full = jnp.repeat(jnp.swapaxes(b_spatial, 1, 2)[:, :, :, None], HEAD_DIM, axis=3)
    bs_full = bs_full.reshape(depth, CHUNK, GROUP_W)
    row3 = lambda a: a.reshape(depth, 1, a.shape[-1])
    gfin = g_final.reshape(1, d)
    e_ids = jnp.arange(N_EXPERTS, dtype=I32)
    blk = jnp.arange(n_blocks, dtype=I32)

    piece_off = GROUP * jnp.arange(PIECES, dtype=I32)
    tile_row0 = LOCAL_ROWS * jnp.arange(n_tiles, dtype=I32)
    zero_row = LOCAL_ROWS - GROUP
    spare_row0 = n_tiles * LOCAL_ROWS
    prev = None
    for l in range(depth):
        mods = [m.reshape(bsz, 1, d) for m in jnp.split(mod[l], N_MOD, axis=-1)]
        shm, scm, gtm, shf, scf, gtf = mods
        x1, xl, lpos8, wgt8, tab3, cnt = _mixer_call(
            l, x, (shm, scm, gtm, shf, scf), row3(g_mix), row3(g_ffn), w_in_b,
            row3(gmlp_ln_g), row3(gmlp_ln_b), w_spatial, bs_full, row3(ret_norm_g), w_out_b,
            cos_t, sin_t, dmat, qdec, kdec, cdec, wr_t, br, prev=prev)

        cntp = cnt[:, 0].astype(I32)
        padded = ((cntp + BM - 1) // BM) * BM
        pend = jnp.sum(jnp.where(e_ids[None, :] <= e_ids[:, None], padded[None, :], 0), axis=1)
        pstart = pend - padded
        n_used = pend[-1] // BM
        block_e = jnp.minimum(
            jnp.sum((pend[None, :] <= (blk * BM)[:, None]).astype(I32), axis=1), N_EXPERTS - 1)
        gbase, runs, lstart = tab3[:, 0, :, 0], tab3[:, 1, :, 0], tab3[:, 2, :, 0]
        r = (blk * BM - pstart[block_e])[:, None] + piece_off[None, :]
        gb = gbase.T[block_e]
        rn = runs.T[block_e]
        delta = tile_row0[None, :] + lstart.T[block_e] - gb
        in_run = (gb[:, None, :] <= r[:, :, None]) & (r[:, :, None] < (gb + rn)[:, None, :])
        row = r + jnp.sum(jnp.where(in_run, delta[:, None, :], 0), axis=-1)
        live = (r < cntp[block_e][:, None]) & (blk < n_used)[:, None]
        src = jnp.where(live, row, zero_row)
        dst = jnp.where(live, row, spare_row0 + (blk % 2)[:, None] * BM + piece_off[None, :])
        piece_tab = jnp.concatenate(
            [src // GROUP, dst // GROUP, jnp.zeros((n_blocks, TAB - 2 * PIECES), I32)],
            axis=1).reshape(-1)

        yl = _ffn_call(block_e + l * N_EXPERTS, n_used.reshape(1), piece_tab, xl,
                       w1_r, b1_r, w2_r, b2_r)
        x = x1
        prev = (yl, lpos8[:TOP_K].T, wgt8[:TOP_K].T, gtf)
    out = _combine_call(prev[0], prev[1], prev[2], x.reshape(n_tok, d), prev[3], gfin, seq)
    return out.reshape(bsz, seq, d)
```

```python
import functools

import jax
import jax.numpy as jnp
from jax import lax
from jax.experimental import pallas as pl
from jax.experimental.pallas import tpu as pltpu

F32 = jnp.float32
BF16 = jnp.bfloat16
I32 = jnp.int32

D_MODEL = 1024
CHUNK = 128
N_HEADS = 4
HEAD_DIM = 128
GROUP_W = N_HEADS * HEAD_DIM
N_EXPERTS = 32
TOP_K = 4
EXPERT_FF = 1024
N_MOD = 6
EPS = 1e-5
ROPE_BASE = 10000.0
SWIGLU_ALPHA = 1.702
SWIGLU_LIMIT = 7.0

TM = 256
BM = 512
GROUP = 8
PAD_ROWS = N_EXPERTS * GROUP
LOCAL_ROWS = TOP_K * TM + PAD_ROWS
PIECES = BM // GROUP
COPY_UNROLL = 8
TAB = 1024
VMEM_LIMIT = 56 * 1024 * 1024


def _gelu_tanh(x):
    return x * (0.5 * (1.0 + jnp.tanh(0.7978845608028654 * (x + 0.044715 * (x * x * x)))))


def _silu(x):
    return x * jax.nn.sigmoid(x)


def _ada_kernel(c_ref, w_ref, b_ref, o_ref):
    ca = _silu(c_ref[...]).astype(BF16)
    o_ref[0] = jnp.dot(ca, w_ref[0].astype(BF16), preferred_element_type=F32) + b_ref[0]


def _ada_call(c, w_ada, b_ada):
    depth, d, _ = w_ada.shape
    bsz = c.shape[0]
    return pl.pallas_call(
        _ada_kernel,
        grid=(depth, N_MOD),
        in_specs=[
            pl.BlockSpec((bsz, d), lambda l, n: (0, 0)),
            pl.BlockSpec((1, d, d), lambda l, n: (l, 0, n)),
            pl.BlockSpec((1, 1, d), lambda l, n: (l, 0, n)),
        ],
        out_specs=pl.BlockSpec((1, bsz, d), lambda l, n: (l, 0, n)),
        out_shape=jax.ShapeDtypeStruct((depth, bsz, N_MOD * d), F32),
        name="adaln_mod",
    )(c, w_ada, b_ada.reshape(depth, 1, N_MOD * d))


def _unpermute_matrix(lpos_tok, w_tok):
    l_io = lax.broadcasted_iota(I32, (TM, LOCAL_ROWS), 1).astype(jnp.int16)
    lp = lpos_tok.astype(jnp.int16)
    wb = w_tok.astype(BF16)
    unperm = jnp.zeros((TM, LOCAL_ROWS), BF16)
    for kk in range(TOP_K):
        unperm = jnp.where(l_io == lp[:, kk:kk + 1], wb[:, kk:kk + 1], unperm)
    return unperm


def _permute_matrix(lpos_rows):
    lr_io = lax.broadcasted_iota(I32, (LOCAL_ROWS, TM), 0).astype(jnp.int16)
    one = jnp.ones((LOCAL_ROWS, TM), BF16)
    perm = jnp.zeros((LOCAL_ROWS, TM), BF16)
    for kk in range(TOP_K):
        perm = jnp.where(lr_io == lpos_rows[kk:kk + 1].astype(jnp.int16), one, perm)
    return perm


def _mixer_kernel(*refs, fuse_prev, nj, n_tiles):
    if fuse_prev:
        (ylp_ref, lposp_ref, wp_ref, lposn_ref, wn_ref, gtp_ref), refs = refs[:6], refs[6:]
        unperm_sc, refs = refs[-1], refs[:-1]
    (x_ref, shm_ref, scm_ref, gtm_ref, shf_ref, scf_ref, gmix_ref, gffn_ref,
     win_ref, lng_ref, lnb_ref, ws_ref, bs_ref, rng_ref, wout_ref,
     cos_ref, sin_ref, dmat_ref, qdec_ref, kdec_ref, cdec_ref, wr_ref, br_ref,
     x1_ref, xl_ref, lpos_ref, wgt_ref, tab_ref, cnt_ref,
     state_ref, base_ref, ob_ref, h2_sc, lpos_sc) = refs
    t = pl.program_id(0)
    live = t < n_tiles
    j = jnp.minimum(t, n_tiles - 1) % nj
    n_chunks = TM // CHUNK

    @pl.when(j == 0)
    def _():
        state_ref[...] = jnp.zeros_like(state_ref)

    @pl.when(t == 0)
    def _():
        base_ref[...] = jnp.zeros_like(base_ref)
        h2_sc[...] = jnp.zeros_like(h2_sc)
        lpos_sc[...] = jnp.zeros_like(lpos_sc)
        if fuse_prev:
            unperm_sc[...] = _unpermute_matrix(lposp_ref[...], wp_ref[...])

    x = x_ref[0]
    if fuse_prev:
        x = x + gtp_ref[0] * jnp.dot(unperm_sc[...], ylp_ref[0].astype(BF16),
                                     preferred_element_type=F32)

    ms = jnp.mean(x * x, axis=-1, keepdims=True)
    h = x * lax.rsqrt(ms + EPS) * (gmix_ref[0] * (1.0 + scm_ref[0])) + shm_ref[0]
    hb = h.astype(BF16)

    def proj(k):
        return jnp.dot(hb, win_ref[0, :, k * GROUP_W:(k + 1) * GROUP_W],
                       preferred_element_type=F32)

    def rows(c):
        return slice(c * CHUNK, (c + 1) * CHUNK)

    def lanes(hh):
        return slice(hh * HEAD_DIM, (hh + 1) * HEAD_DIM)

    gu = _gelu_tanh(proj(0))
    gv = _gelu_tanh(proj(1))
    mu = jnp.mean(gv, axis=-1, keepdims=True)
    vc = gv - mu
    var = jnp.mean(vc * vc, axis=-1, keepdims=True)
    vn = (vc * lax.rsqrt(var + EPS) * lng_ref[0] + lnb_ref[0]).astype(BF16)
    t_io = lax.broadcasted_iota(I32, (CHUNK, CHUNK), 0)
    s_io = lax.broadcasted_iota(I32, (CHUNK, CHUNK), 1)
    causal = t_io >= s_io
    bias = bs_ref[0]
    for hh in range(N_HEADS):
        wm = jnp.where(causal, ws_ref[0, hh], 0.0).astype(BF16)
        for c in range(n_chunks):
            mixed = jnp.dot(wm, vn[rows(c), lanes(hh)], preferred_element_type=F32)
            oa = gu[rows(c), lanes(hh)] * (mixed + bias[:, lanes(hh)])
            ob_ref[rows(c), lanes(hh)] = oa.astype(BF16)

    q = proj(2)
    k = proj(3)
    vr = proj(4).astype(BF16)
    g = proj(5)
    cos = cos_ref[...]
    sin = sin_ref[...]
    k_scale = HEAD_DIM ** -0.5
    for hh in range(N_HEADS):
        qh = q[:, lanes(hh)]
        kh = k[:, lanes(hh)]
        qh = qh * cos + pltpu.roll(qh, HEAD_DIM // 2, 1) * sin
        kh = (kh * cos + pltpu.roll(kh, HEAD_DIM // 2, 1) * sin) * k_scale
        dm = dmat_ref[hh]
        qd = qdec_ref[:, lanes(hh)]
        kd = kdec_ref[:, lanes(hh)]
        cd = cdec_ref[:, lanes(hh)]
        gn = rng_ref[0][:, lanes(hh)]
        st = state_ref[hh]
        for c in range(n_chunks):
            qc = qh[rows(c)]
            kc = kh[rows(c)]
            vcb = vr[rows(c), lanes(hh)]
            s = lax.dot_general(qc.astype(BF16), kc.astype(BF16),
                                (((1,), (1,)), ((), ())), preferred_element_type=F32) * dm
            o = jnp.dot(s.astype(BF16), vcb, preferred_element_type=F32)
            o = o + jnp.dot((qc * qd).astype(BF16), st.astype(BF16), preferred_element_type=F32)
            kdt = (kc * kd).T.astype(BF16)
            st = st * cd + jnp.dot(kdt, vcb, preferred_element_type=F32)
            o = o * lax.rsqrt(jnp.mean(o * o, axis=-1, keepdims=True) + EPS) * gn
            ob = o * _silu(g[rows(c), lanes(hh)])
            ob_ref[rows(c), GROUP_W + hh * HEAD_DIM:GROUP_W + (hh + 1) * HEAD_DIM] = ob.astype(BF16)
        state_ref[hh] = st

    y = jnp.dot(ob_ref[...], wout_ref[0], preferred_element_type=F32)
    x1 = x + gtm_ref[0] * y

    ms2 = jnp.mean(x1 * x1, axis=-1, keepdims=True)
    h2 = x1 * lax.rsqrt(ms2 + EPS) * (gffn_ref[0] * (1.0 + scf_ref[0])) + shf_ref[0]
    h2b = h2.astype(BF16)

    if fuse_prev:
        unperm_sc[...] = _unpermute_matrix(lposn_ref[...], wn_ref[...])

    lg = lax.dot_general(wr_ref[0], h2b, (((1,), (1,)), ((), ())),
                         preferred_element_type=F32) + br_ref[0]
    e_io = lax.broadcasted_iota(I32, (N_EXPERTS, TM), 0)
    work = lg
    vals, ids = [], []
    for _ in range(TOP_K):
        m = jnp.max(work, axis=0, keepdims=True)
        ik = jnp.min(jnp.where(work == m, e_io, N_EXPERTS), axis=0, keepdims=True)
        vals.append(m)
        ids.append(ik)
        work = jnp.where(e_io == ik, -jnp.inf, work)
    exps = [jnp.exp(v - vals[0]) for v in vals]
    denom = exps[0] + exps[1] + exps[2] + exps[3]
    wts = [e / denom for e in exps]

    xl_ref[0] = jnp.dot(_permute_matrix(lpos_sc[...]), h2_sc[...], preferred_element_type=F32)

    sel = [e_io == ik for ik in ids]
    oh = jnp.where(sel[0] | sel[1] | sel[2] | sel[3], 1.0, 0.0)
    r_io = lax.broadcasted_iota(I32, (TM, TM), 0)
    c_io = lax.broadcasted_iota(I32, (TM, TM), 1)
    upper = jnp.where(r_io < c_io, 1.0, 0.0).astype(BF16)
    cum = jnp.dot(oh.astype(BF16), upper, preferred_element_type=F32)
    cnt_t = jnp.sum(oh, axis=1, keepdims=True)
    run = jnp.ceil(cnt_t * (1.0 / GROUP)) * GROUP
    run_b = jnp.broadcast_to(run, (N_EXPERTS, TM))
    er_io = lax.broadcasted_iota(I32, (N_EXPERTS, N_EXPERTS), 0)
    ec_io = lax.broadcasted_iota(I32, (N_EXPERTS, N_EXPERTS), 1)
    lower = jnp.where(ec_io < er_io, 1.0, 0.0).astype(BF16)
    lstart = jnp.dot(lower, run_b.astype(BF16), preferred_element_type=F32)
    pos = lstart + cum
    lpos = [jnp.sum(jnp.where(sk, pos, 0.0), axis=0, keepdims=True) for sk in sel]
    gbase = base_ref[...]
    row8 = lax.broadcasted_iota(I32, (8, TM), 0)

    def stack4(parts, fill):
        out = jnp.full((8, TM), fill, parts[0].dtype)
        for kk in range(TOP_K):
            out = jnp.where(row8 == kk, parts[kk], out)
        return out

    lpos8 = stack4([p.astype(I32) for p in lpos], 0)
    wgt8 = stack4(wts, 0.0)

    @pl.when(live)
    def _():
        x1_ref[0] = x1
        base_ref[...] = gbase + run_b
        cnt_ref[...] = (gbase + run_b)[:, :128]
        tab_ref[0, 0] = gbase[:, :128].astype(I32)
        tab_ref[0, 1] = run_b[:, :128].astype(I32)
        tab_ref[0, 2] = lstart[:, :128].astype(I32)
        lpos_ref[...] = lpos8
        wgt_ref[...] = wgt8
        h2_sc[...] = h2b
        lpos_sc[...] = lpos8


def _mixer_call(layer, x, mods, g_mix, g_ffn, w_in_b, ln_g, ln_b, w_sp, bs_full, rn_g,
                w_out_b, cos_t, sin_t, dmat, qdec, kdec, cdec, wr_t, br, prev=None):
    bsz, seq, d = x.shape
    n_tok = bsz * seq
    nj = seq // TM
    n_tiles = bsz * nj
    shm, scm, gtm, shf, scf = mods
    tile = lambda t: jnp.minimum(t, n_tiles - 1)
    vec = lambda: pl.BlockSpec((1, 1, d), lambda t: (tile(t) // nj, 0, 0))
    lvec = lambda w: pl.BlockSpec((1, 1, w), lambda t: (layer, 0, 0))
    full = lambda shp: pl.BlockSpec(shp, lambda t: (0,) * len(shp))
    tok = lambda shp: pl.BlockSpec(shp, lambda t: (0, tile(t)))
    in_specs = [
        pl.BlockSpec((1, TM, d), lambda t: (tile(t) // nj, tile(t) % nj, 0)),
        vec(), vec(), vec(), vec(), vec(),
        lvec(d), lvec(d),
        pl.BlockSpec((1, d, 6 * GROUP_W), lambda t: (layer, 0, 0)),
        lvec(GROUP_W), lvec(GROUP_W),
        pl.BlockSpec((1, N_HEADS, CHUNK, CHUNK), lambda t: (layer, 0, 0, 0)),
        pl.BlockSpec((1, CHUNK, GROUP_W), lambda t: (layer, 0, 0)),
        lvec(GROUP_W),
        pl.BlockSpec((1, d, d), lambda t: (layer, 0, 0)),
        pl.BlockSpec((TM, HEAD_DIM), lambda t: (tile(t) % nj, 0)),
        pl.BlockSpec((TM, HEAD_DIM), lambda t: (tile(t) % nj, 0)),
        full((N_HEADS, CHUNK, CHUNK)),
        full((CHUNK, GROUP_W)), full((CHUNK, GROUP_W)), full((1, GROUP_W)),
        pl.BlockSpec((1, N_EXPERTS, d), lambda t: (layer, 0, 0)),
        pl.BlockSpec((1, N_EXPERTS, 1), lambda t: (layer, 0, 0)),
    ]
    out_specs = [
        pl.BlockSpec((1, TM, d), lambda t: (tile(t) // nj, tile(t) % nj, 0)),
        pl.BlockSpec((1, LOCAL_ROWS, d), lambda t: (jnp.maximum(t - 1, 0), 0, 0)),
        tok((8, TM)), tok((8, TM)),
        pl.BlockSpec((1, 3, N_EXPERTS, 128), lambda t: (tile(t), 0, 0, 0)),
        pl.BlockSpec((N_EXPERTS, 128), lambda t: (0, 0)),
    ]
    out_shape = [
        jax.ShapeDtypeStruct((bsz, seq, d), F32),
        jax.ShapeDtypeStruct((n_tiles, LOCAL_ROWS, d), F32),
        jax.ShapeDtypeStruct((8, n_tok), I32),
        jax.ShapeDtypeStruct((8, n_tok), F32),
        jax.ShapeDtypeStruct((n_tiles, 3, N_EXPERTS, 128), I32),
        jax.ShapeDtypeStruct((N_EXPERTS, 128), F32),
    ]
    args = (x, shm, scm, gtm, shf, scf, g_mix, g_ffn, w_in_b, ln_g, ln_b, w_sp, bs_full, rn_g,
            w_out_b, cos_t, sin_t, dmat, qdec, kdec, cdec, wr_t, br)
    scratch = [
        pltpu.VMEM((N_HEADS, HEAD_DIM, HEAD_DIM), F32),
        pltpu.VMEM((N_EXPERTS, TM), F32),
        pltpu.VMEM((TM, d), BF16),
        pltpu.VMEM((TM, d), BF16),
        pltpu.VMEM((8, TM), I32),
    ]
    if prev is not None:
        yl, lpos_tok, w_tok, gtf = prev
        nxt = lambda t: jnp.minimum(t + 1, n_tiles - 1)
        in_specs = [
            pl.BlockSpec((1, LOCAL_ROWS, d), lambda t: (tile(t), 0, 0)),
            pl.BlockSpec((TM, TOP_K), lambda t: (tile(t), 0)),
            pl.BlockSpec((TM, TOP_K), lambda t: (tile(t), 0)),
            pl.BlockSpec((TM, TOP_K), lambda t: (nxt(t), 0)),
            pl.BlockSpec((TM, TOP_K), lambda t: (nxt(t), 0)),
            vec(),
        ] + in_specs
        args = (yl, lpos_tok, w_tok, lpos_tok, w_tok, gtf) + args
        scratch.append(pltpu.VMEM((TM, LOCAL_ROWS), BF16))
    return pl.pallas_call(
        functools.partial(_mixer_kernel, fuse_prev=prev is not None, nj=nj, n_tiles=n_tiles),
        grid=(n_tiles + 1,),
        in_specs=in_specs,
        out_specs=out_specs,
        out_shape=out_shape,
        scratch_shapes=scratch,
        compiler_params=pltpu.CompilerParams(
            dimension_semantics=("arbitrary",), vmem_limit_bytes=VMEM_LIMIT),
        name="mixer_router",
    )(*args)


def _ffn_kernel(be_ref, nused_ref, tab_ref, tabn_ref, w1_ref, b1_ref, w2_ref, b2_ref,
                xl_hbm, yl_hbm, xbuf, ybuf, zbuf, w1b, w2b, sem_in, sem_out, zsem):
    i = pl.program_id(0)
    par = i % 2
    n_used = nused_ref[0]
    tile_pieces = LOCAL_ROWS // GROUP
    pad_pieces = PAD_ROWS // GROUP
    n_tiles = yl_hbm.shape[0] // tile_pieces - 1
    d = xbuf.shape[-1]

    @pl.when((i < n_used) & ((i == 0) | (be_ref[i] != be_ref[jnp.maximum(i - 1, 0)])))
    def _():
        w1b[...] = w1_ref[0].astype(BF16)
        w2b[...] = w2_ref[0].astype(BF16)

    def zero_copy(piece0):
        return pltpu.make_async_copy(zbuf, yl_hbm.at[pl.ds(piece0, pad_pieces)], zsem)

    n_zero = n_tiles + tile_pieces // pad_pieces

    def zero_piece0(z):
        return jnp.where(z < n_tiles, z * tile_pieces + TOP_K * TM // GROUP,
                         n_tiles * tile_pieces + (z - n_tiles) * pad_pieces)

    @pl.when(i == 0)
    def _():
        zbuf[...] = jnp.zeros_like(zbuf)

        def body(z, c):
            zero_copy(zero_piece0(z)).start()
            return c
        lax.fori_loop(0, n_zero, body, 0)

    def start_fetch(tab, slot):
        def body(g, c):
            for u in range(COPY_UNROLL):
                p = g * COPY_UNROLL + u
                pltpu.make_async_copy(xl_hbm.at[tab[p]], xbuf.at[slot, p],
                                      sem_in.at[slot]).start(priority=u % 2)
            return c
        lax.fori_loop(0, PIECES // COPY_UNROLL, body, 0)

    def wait_fetch(slot):
        pltpu.make_async_copy(xl_hbm.at[pl.ds(0, PIECES)], xbuf.at[slot], sem_in.at[slot]).wait()

    def wait_store(slot):
        pltpu.make_async_copy(ybuf.at[slot], yl_hbm.at[pl.ds(0, PIECES)], sem_out.at[slot]).wait()

    @pl.when(i == 0)
    def _():
        start_fetch(tab_ref, 0)

    @pl.when(i + 1 < n_used)
    def _():
        start_fetch(tabn_ref, 1 - par)

    @pl.when(i < n_used)
    def _():
        wait_fetch(par)

        @pl.when(i >= 2)
        def _():
            wait_store(par)

        xb = xbuf[par].reshape(BM, d).astype(BF16)
        a = jnp.dot(xb, w1b[...], preferred_element_type=F32) + b1_ref[0]
        a_glu = jnp.minimum(a[:, :EXPERT_FF], SWIGLU_LIMIT)
        a_lin = jnp.clip(a[:, EXPERT_FF:], -SWIGLU_LIMIT, SWIGLU_LIMIT)
        act = a_glu * jax.nn.sigmoid(SWIGLU_ALPHA * a_glu) * (a_lin + 1.0)
        yb = jnp.dot(act.astype(BF16), w2b[...], preferred_element_type=F32) + b2_ref[0]
        ybuf[par] = yb.reshape(PIECES, GROUP, d)

        @pl.when(i == 0)
        def _():
            def body(z, c):
                zero_copy(0).wait()
                return c
            lax.fori_loop(0, n_zero, body, 0)

        def store(g, c):
            for u in range(COPY_UNROLL):
                p = g * COPY_UNROLL + u
                pltpu.make_async_copy(ybuf.at[par, p], yl_hbm.at[tab_ref[PIECES + p]],
                                      sem_out.at[par]).start(priority=u % 2)
            return c
        lax.fori_loop(0, PIECES // COPY_UNROLL, store, 0)

        @pl.when(i == n_used - 1)
        def _():
            wait_store(par)

            @pl.when(i >= 1)
            def _():
                wait_store(1 - par)


def _ffn_call(block_expert, n_used, piece_tab, xl, w1, b1, w2, b2):
    n_tiles, _, d = xl.shape
    n_blocks = piece_tab.shape[0] // TAB
    tile_pieces = LOCAL_ROWS // GROUP
    xl3 = xl.reshape(n_tiles * tile_pieces, GROUP, d)
    yl_shape = (n_tiles + 1, LOCAL_ROWS, d)
    grid_spec = pltpu.PrefetchScalarGridSpec(
        num_scalar_prefetch=2,
        grid=(n_blocks,),
        in_specs=[
            pl.BlockSpec((TAB,), lambda i, be, nu: (i,), memory_space=pltpu.SMEM),
            pl.BlockSpec((TAB,), lambda i, be, nu: (jnp.minimum(i + 1, n_blocks - 1),),
                         memory_space=pltpu.SMEM),
            pl.BlockSpec((1, d, 2 * EXPERT_FF), lambda i, be, nu: (be[i], 0, 0)),
            pl.BlockSpec((1, 1, 2 * EXPERT_FF), lambda i, be, nu: (be[i], 0, 0)),
            pl.BlockSpec((1, EXPERT_FF, d), lambda i, be, nu: (be[i], 0, 0)),
            pl.BlockSpec((1, 1, d), lambda i, be, nu: (be[i], 0, 0)),
            pl.BlockSpec(memory_space=pl.ANY),
        ],
        out_specs=pl.BlockSpec(memory_space=pl.ANY),
        scratch_shapes=[pltpu.VMEM((2, PIECES, GROUP, d), F32), pltpu.VMEM((2, PIECES, GROUP, d), F32),
                        pltpu.VMEM((PAD_ROWS // GROUP, GROUP, d), F32),
                        pltpu.VMEM((d, 2 * EXPERT_FF), BF16), pltpu.VMEM((EXPERT_FF, d), BF16),
                        pltpu.SemaphoreType.DMA((2,)), pltpu.SemaphoreType.DMA((2,)),
                        pltpu.SemaphoreType.DMA],
    )
    out = pl.pallas_call(
        _ffn_kernel,
        grid_spec=grid_spec,
        out_shape=jax.ShapeDtypeStruct(((n_tiles + 1) * tile_pieces, GROUP, d), F32),
        compiler_params=pltpu.CompilerParams(
            dimension_semantics=("arbitrary",), vmem_limit_bytes=VMEM_LIMIT),
        name="moe_ffn",
    )(block_expert, n_used, piece_tab, piece_tab, w1, b1, w2, b2, xl3)
    return out.reshape(yl_shape)


def _combine_kernel(yl_ref, lpos_ref, w_ref, x1_ref, gt_ref, gfin_ref, o_ref):
    acc = jnp.dot(_unpermute_matrix(lpos_ref[...], w_ref[...]), yl_ref[0].astype(BF16),
                  preferred_element_type=F32)
    out = x1_ref[...] + gt_ref[0] * acc
    o_ref[...] = out * lax.rsqrt(jnp.mean(out * out, axis=-1, keepdims=True) + EPS) * gfin_ref[...]


def _combine_call(yl, lpos_tok, w_tok, x1, gtf, g_final, seq):
    n_tok, d = x1.shape
    per_b = seq // TM
    return pl.pallas_call(
        _combine_kernel,
        grid=(n_tok // TM,),
        in_specs=[
            pl.BlockSpec((1, LOCAL_ROWS, d), lambda i: (i, 0, 0)),
            pl.BlockSpec((TM, TOP_K), lambda i: (i, 0)),
            pl.BlockSpec((TM, TOP_K), lambda i: (i, 0)),
            pl.BlockSpec((TM, d), lambda i: (i, 0)),
            pl.BlockSpec((1, 1, d), lambda i: (i // per_b, 0, 0)),
            pl.BlockSpec((1, d), lambda i: (0, 0)),
        ],
        out_specs=pl.BlockSpec((TM, d), lambda i: (i, 0)),
        out_shape=jax.ShapeDtypeStruct((n_tok, d), F32),
        compiler_params=pltpu.CompilerParams(
            dimension_semantics=("arbitrary",), vmem_limit_bytes=VMEM_LIMIT),
        name="moe_combine",
    )(yl, lpos_tok, w_tok, x1, gtf, g_final)


def _tables(seq):
    half = HEAD_DIM // 2
    inv_freq = ROPE_BASE ** (-jnp.arange(half, dtype=F32) / half)
    ang = jnp.arange(seq, dtype=F32)[:, None] * inv_freq[None, :]
    cos = jnp.cos(ang)
    sin = jnp.sin(ang)
    cos_t = jnp.concatenate([cos, cos], axis=-1)
    sin_t = jnp.concatenate([-sin, sin], axis=-1)
    gamma = 1.0 - jnp.exp2(-5.0 - jnp.arange(N_HEADS, dtype=F32))
    log_g = jnp.log(gamma)
    idx = jnp.arange(CHUNK, dtype=F32)
    diff = idx[:, None] - idx[None, :]
    dmat = jnp.where(diff[None] >= 0,
                     jnp.exp(log_g[:, None, None] * jnp.maximum(diff, 0.0)[None]), 0.0)
    k_decay = jnp.exp(log_g[:, None] * (CHUNK - 1.0 - idx)[None, :])
    q_decay = jnp.exp(log_g[:, None] * (idx + 1.0)[None, :])
    chunk_decay = jnp.exp(log_g * CHUNK)
    widen = lambda t: jnp.repeat(t.T[:, :, None], HEAD_DIM, axis=2).reshape(CHUNK, GROUP_W)
    qdec = widen(q_decay)
    kdec = widen(k_decay)
    cdec = jnp.repeat(chunk_decay[:, None], HEAD_DIM, axis=1).reshape(1, GROUP_W)
    return cos_t, sin_t, dmat, qdec, kdec, cdec


def kernel(x, c, w_ada, b_ada, g_mix, w_in, gmlp_ln_g, gmlp_ln_b, w_spatial, b_spatial,
           ret_norm_g, w_out, g_ffn, w_router, b_router, w1, b1, w2, b2, g_final):
    bsz, seq, d = x.shape
    depth = w_in.shape[0]
    n_tok = bsz * seq
    n_tiles = n_tok // TM
    n_slots = n_tok * TOP_K + n_tiles * N_EXPERTS * GROUP + N_EXPERTS * BM
    assert n_slots % BM == 0
    n_blocks = n_slots // BM

    mod = _ada_call(c, w_ada, b_ada)
    cos_t, sin_t, dmat, qdec, kdec, cdec = _tables(seq)

    w_in_b = w_in.astype(BF16)
    w_out_b = w_out.astype(BF16)
    wr_t = jnp.swapaxes(w_router, 1, 2).astype(BF16)
    br = b_router.reshape(depth, N_EXPERTS, 1)
    w1_r = w1.reshape(depth * N_EXPERTS, d, 2 * EXPERT_FF)
    w2_r = w2.reshape(depth * N_EXPERTS, EXPERT_FF, d)
    b1_r = b1.reshape(depth * N_EXPERTS, 1, 2 * EXPERT_FF)
    b2_r = b2.reshape(depth * N_EXPERTS, 1, d)
    bs_full = jnp.repeat(jnp.swapaxes(b_spatial, 1, 2)[:, :, :, None], HEAD_DIM, axis=3)
    bs_full = bs_full.reshape(depth, CHUNK, GROUP_W)
    row3 = lambda a: a.reshape(depth, 1, a.shape[-1])
    gfin = g_final.reshape(1, d)
    e_ids = jnp.arange(N_EXPERTS, dtype=I32)
    blk = jnp.arange(n_blocks, dtype=I32)

    piece_off = GROUP * jnp.arange(PIECES, dtype=I32)
    tile_row0 = LOCAL_ROWS * jnp.arange(n_tiles, dtype=I32)
    zero_row = LOCAL_ROWS - GROUP
    spare_row0 = n_tiles * LOCAL_ROWS
    prev = None
    for l in range(depth):
        mods = [m.reshape(bsz, 1, d) for m in jnp.split(mod[l], N_MOD, axis=-1)]
        shm, scm, gtm, shf, scf, gtf = mods
        x1, xl, lpos8, wgt8, tab3, cnt = _mixer_call(
            l, x, (shm, scm, gtm, shf, scf), row3(g_mix), row3(g_ffn), w_in_b,
            row3(gmlp_ln_g), row3(gmlp_ln_b), w_spatial, bs_full, row3(ret_norm_g), w_out_b,
            cos_t, sin_t, dmat, qdec, kdec, cdec, wr_t, br, prev=prev)

        cntp = cnt[:, 0].astype(I32)
        padded = ((cntp + BM - 1) // BM) * BM
        pend = jnp.sum(jnp.where(e_ids[None, :] <= e_ids[:, None], padded[None, :], 0), axis=1)
        pstart = pend - padded
        n_used = pend[-1] // BM
        block_e = jnp.minimum(
            jnp.sum((pend[None, :] <= (blk * BM)[:, None]).astype(I32), axis=1), N_EXPERTS - 1)
        gbase, runs, lstart = tab3[:, 0, :, 0], tab3[:, 1, :, 0], tab3[:, 2, :, 0]
        r = (blk * BM - pstart[block_e])[:, None] + piece_off[None, :]
        gb = gbase.T[block_e]
        rn = runs.T[block_e]
        delta = tile_row0[None, :] + lstart.T[block_e] - gb
        in_run = (gb[:, None, :] <= r[:, :, None]) & (r[:, :, None] < (gb + rn)[:, None, :])
        row = r + jnp.sum(jnp.where(in_run, delta[:, None, :], 0), axis=-1)
        live = (r < cntp[block_e][:, None]) & (blk < n_used)[:, None]
        src = jnp.where(live, row, zero_row)
        dst = jnp.where(live, row, spare_row0 + (blk % 2)[:, None] * BM + piece_off[None, :])
        piece_tab = jnp.concatenate(
            [src // GROUP, dst // GROUP, jnp.zeros((n_blocks, TAB - 2 * PIECES), I32)],
            axis=1).reshape(-1)

        yl = _ffn_call(block_e + l * N_EXPERTS, n_used.reshape(1), piece_tab, xl,
                       w1_r, b1_r, w2_r, b2_r)
        x = x1
        prev = (yl, lpos8[:TOP_K].T, wgt8[:TOP_K].T, gtf)
    out = _combine_call(prev[0], prev[1], prev[2], x.reshape(n_tok, d), prev[3], gfin, seq)
    return out.reshape(bsz, seq, d)
```

```python
import functools

import jax
import jax.numpy as jnp
from jax import lax
from jax.experimental import pallas as pl
from jax.experimental.pallas import tpu as pltpu

F32 = jnp.float32
BF16 = jnp.bfloat16
I32 = jnp.int32

D_MODEL = 1024
CHUNK = 128
N_HEADS = 4
HEAD_DIM = 128
GROUP_W = N_HEADS * HEAD_DIM
N_EXPERTS = 32
TOP_K = 4
EXPERT_FF = 1024
N_MOD = 6
EPS = 1e-5
ROPE_BASE = 10000.0
SWIGLU_ALPHA = 1.702
SWIGLU_LIMIT = 7.0

TM = 256
BM = 512
GROUP = 8
PAD_ROWS = N_EXPERTS * GROUP
LOCAL_ROWS = TOP_K * TM + PAD_ROWS
PIECES = BM // GROUP
COPY_UNROLL = 8
TAB = 1024
VMEM_LIMIT = 56 * 1024 * 1024


def _gelu_tanh(x):
    return x * (0.5 * (1.0 + jnp.tanh(0.7978845608028654 * (x + 0.044715 * (x * x * x)))))


def _silu(x):
    return x * jax.nn.sigmoid(x)


def _ada_kernel(c_ref, w_ref, b_ref, o_ref):
    ca = _silu(c_ref[...]).astype(BF16)
    o_ref[0] = jnp.dot(ca, w_ref[0].astype(BF16), preferred_element_type=F32) + b_ref[0]


def _ada_call(c, w_ada, b_ada):
    depth, d, _ = w_ada.shape
    bsz = c.shape[0]
    return pl.pallas_call(
        _ada_kernel,
        grid=(depth, N_MOD),
        in_specs=[
            pl.BlockSpec((bsz, d), lambda l, n: (0, 0)),
            pl.BlockSpec((1, d, d), lambda l, n: (l, 0, n)),
            pl.BlockSpec((1, 1, d), lambda l, n: (l, 0, n)),
        ],
        out_specs=pl.BlockSpec((1, bsz, d), lambda l, n: (l, 0, n)),
        out_shape=jax.ShapeDtypeStruct((depth, bsz, N_MOD * d), F32),
        name="adaln_mod",
    )(c, w_ada, b_ada.reshape(depth, 1, N_MOD * d))


def _unpermute_matrix(lpos_tok, w_tok):
    l_io = lax.broadcasted_iota(I32, (TM, LOCAL_ROWS), 1).astype(jnp.int16)
    lp = lpos_tok.astype(jnp.int16)
    wb = w_tok.astype(BF16)
    unperm = jnp.zeros((TM, LOCAL_ROWS), BF16)
    for kk in range(TOP_K):
        unperm = jnp.where(l_io == lp[:, kk:kk + 1], wb[:, kk:kk + 1], unperm)
    return unperm


def _pack_bf16_pairs(y):
    half = y.shape[-1] // 2
    hi = lax.bitcast_convert_type(y[:, :half].astype(BF16).astype(F32), jnp.uint32)
    lo = lax.bitcast_convert_type(y[:, half:].astype(BF16).astype(F32), jnp.uint32)
    return hi | (lo >> 16)


def _unpack_bf16_pairs(u):
    hi = lax.bitcast_convert_type(u & jnp.uint32(0xFFFF0000), F32).astype(BF16)
    lo = lax.bitcast_convert_type(u << 16, F32).astype(BF16)
    return jnp.concatenate([hi, lo], axis=-1)


def _permute_matrix(lpos_rows):
    lr_io = lax.broadcasted_iota(I32, (LOCAL_ROWS, TM), 0).astype(jnp.int16)
    one = jnp.ones((LOCAL_ROWS, TM), BF16)
    perm = jnp.zeros((LOCAL_ROWS, TM), BF16)
    for kk in range(TOP_K):
        perm = jnp.where(lr_io == lpos_rows[kk:kk + 1].astype(jnp.int16), one, perm)
    return perm


def _mixer_kernel(*refs, fuse_prev, nj, n_tiles):
    if fuse_prev:
        (ylp_ref, lposp_ref, wp_ref, lposn_ref, wn_ref, gtp_ref), refs = refs[:6], refs[6:]
        unperm_sc, refs = refs[-1], refs[:-1]
    (x_ref, shm_ref, scm_ref, gtm_ref, shf_ref, scf_ref, gmix_ref, gffn_ref,
     win_ref, lng_ref, lnb_ref, ws_ref, bs_ref, rng_ref, wout_ref,
     cos_ref, sin_ref, dmat_ref, qdec_ref, kdec_ref, cdec_ref, wr_ref, br_ref,
     x1_ref, xl_ref, lpos_ref, wgt_ref, tab_ref, cnt_ref,
     state_ref, base_ref, ob_ref, h2_sc, lpos_sc) = refs
    t = pl.program_id(0)
    live = t < n_tiles
    j = jnp.minimum(t, n_tiles - 1) % nj
    n_chunks = TM // CHUNK

    @pl.when(j == 0)
    def _():
        state_ref[...] = jnp.zeros_like(state_ref)

    @pl.when(t == 0)
    def _():
        base_ref[...] = jnp.zeros_like(base_ref)
        h2_sc[...] = jnp.zeros_like(h2_sc)
        lpos_sc[...] = jnp.zeros_like(lpos_sc)
        if fuse_prev:
            unperm_sc[...] = _unpermute_matrix(lposp_ref[...], wp_ref[...])

    x = x_ref[0]
    if fuse_prev:
        x = x + gtp_ref[0] * jnp.dot(unperm_sc[...], _unpack_bf16_pairs(ylp_ref[0]),
                                     preferred_element_type=F32)

    ms = jnp.mean(x * x, axis=-1, keepdims=True)
    h = x * lax.rsqrt(ms + EPS) * (gmix_ref[0] * (1.0 + scm_ref[0])) + shm_ref[0]
    hb = h.astype(BF16)

    def proj(k):
        return jnp.dot(hb, win_ref[0, :, k * GROUP_W:(k + 1) * GROUP_W],
                       preferred_element_type=F32)

    def rows(c):
        return slice(c * CHUNK, (c + 1) * CHUNK)

    def lanes(hh):
        return slice(hh * HEAD_DIM, (hh + 1) * HEAD_DIM)

    gu = _gelu_tanh(proj(0))
    gv = _gelu_tanh(proj(1))
    mu = jnp.mean(gv, axis=-1, keepdims=True)
    vc = gv - mu
    var = jnp.mean(vc * vc, axis=-1, keepdims=True)
    vn = (vc * lax.rsqrt(var + EPS) * lng_ref[0] + lnb_ref[0]).astype(BF16)
    t_io = lax.broadcasted_iota(I32, (CHUNK, CHUNK), 0)
    s_io = lax.broadcasted_iota(I32, (CHUNK, CHUNK), 1)
    causal = t_io >= s_io
    bias = bs_ref[0]
    for hh in range(N_HEADS):
        wm = jnp.where(causal, ws_ref[0, hh], 0.0).astype(BF16)
        for c in range(n_chunks):
            mixed = jnp.dot(wm, vn[rows(c), lanes(hh)], preferred_element_type=F32)
            oa = gu[rows(c), lanes(hh)] * (mixed + bias[:, lanes(hh)])
            ob_ref[rows(c), lanes(hh)] = oa.astype(BF16)

    q = proj(2)
    k = proj(3)
    vr = proj(4).astype(BF16)
    g = proj(5)
    cos = cos_ref[...]
    sin = sin_ref[...]
    k_scale = HEAD_DIM ** -0.5
    for hh in range(N_HEADS):
        qh = q[:, lanes(hh)]
        kh = k[:, lanes(hh)]
        qh = qh * cos + pltpu.roll(qh, HEAD_DIM // 2, 1) * sin
        kh = (kh * cos + pltpu.roll(kh, HEAD_DIM // 2, 1) * sin) * k_scale
        dm = dmat_ref[hh]
        qd = qdec_ref[:, lanes(hh)]
        kd = kdec_ref[:, lanes(hh)]
        cd = cdec_ref[:, lanes(hh)]
        gn = rng_ref[0][:, lanes(hh)]
        st = state_ref[hh]
        for c in range(n_chunks):
            qc = qh[rows(c)]
            kc = kh[rows(c)]
            vcb = vr[rows(c), lanes(hh)]
            s = lax.dot_general(qc.astype(BF16), kc.astype(BF16),
                                (((1,), (1,)), ((), ())), preferred_element_type=F32) * dm
            o = jnp.dot(s.astype(BF16), vcb, preferred_element_type=F32)
            o = o + jnp.dot((qc * qd).astype(BF16), st.astype(BF16), preferred_element_type=F32)
            kdt = (kc * kd).T.astype(BF16)
            st = st * cd + jnp.dot(kdt, vcb, preferred_element_type=F32)
            o = o * lax.rsqrt(jnp.mean(o * o, axis=-1, keepdims=True) + EPS) * gn
            ob = o * _silu(g[rows(c), lanes(hh)])
            ob_ref[rows(c), GROUP_W + hh * HEAD_DIM:GROUP_W + (hh + 1) * HEAD_DIM] = ob.astype(BF16)
        state_ref[hh] = st

    y = jnp.dot(ob_ref[...], wout_ref[0], preferred_element_type=F32)
    x1 = x + gtm_ref[0] * y

    ms2 = jnp.mean(x1 * x1, axis=-1, keepdims=True)
    h2 = x1 * lax.rsqrt(ms2 + EPS) * (gffn_ref[0] * (1.0 + scf_ref[0])) + shf_ref[0]
    h2b = h2.astype(BF16)

    if fuse_prev:
        unperm_sc[...] = _unpermute_matrix(lposn_ref[...], wn_ref[...])

    lg = lax.dot_general(wr_ref[0], h2b, (((1,), (1,)), ((), ())),
                         preferred_element_type=F32) + br_ref[0]
    e_io = lax.broadcasted_iota(I32, (N_EXPERTS, TM), 0)
    work = lg
    vals, ids = [], []
    for _ in range(TOP_K):
        m = jnp.max(work, axis=0, keepdims=True)
        ik = jnp.min(jnp.where(work == m, e_io, N_EXPERTS), axis=0, keepdims=True)
        vals.append(m)
        ids.append(ik)
        work = jnp.where(e_io == ik, -jnp.inf, work)
    exps = [jnp.exp(v - vals[0]) for v in vals]
    denom = exps[0] + exps[1] + exps[2] + exps[3]
    wts = [e / denom for e in exps]

    xl_ref[0] = jnp.dot(_permute_matrix(lpos_sc[...]), h2_sc[...], preferred_element_type=F32)

    sel = [e_io == ik for ik in ids]
    oh = jnp.where(sel[0] | sel[1] | sel[2] | sel[3], 1.0, 0.0)
    r_io = lax.broadcasted_iota(I32, (TM, TM), 0)
    c_io = lax.broadcasted_iota(I32, (TM, TM), 1)
    upper = jnp.where(r_io < c_io, 1.0, 0.0).astype(BF16)
    cum = jnp.dot(oh.astype(BF16), upper, preferred_element_type=F32)
    cnt_t = jnp.sum(oh, axis=1, keepdims=True)
    run = jnp.ceil(cnt_t * (1.0 / GROUP)) * GROUP
    run_b = jnp.broadcast_to(run, (N_EXPERTS, TM))
    er_io = lax.broadcasted_iota(I32, (N_EXPERTS, N_EXPERTS), 0)
    ec_io = lax.broadcasted_iota(I32, (N_EXPERTS, N_EXPERTS), 1)
    lower = jnp.where(ec_io < er_io, 1.0, 0.0).astype(BF16)
    lstart = jnp.dot(lower, run_b.astype(BF16), preferred_element_type=F32)
    pos = lstart + cum
    lpos = [jnp.sum(jnp.where(sk, pos, 0.0), axis=0, keepdims=True) for sk in sel]
    gbase = base_ref[...]
    row8 = lax.broadcasted_iota(I32, (8, TM), 0)

    def stack4(parts, fill):
        out = jnp.full((8, TM), fill, parts[0].dtype)
        for kk in range(TOP_K):
            out = jnp.where(row8 == kk, parts[kk], out)
        return out

    lpos8 = stack4([p.astype(I32) for p in lpos], 0)
    wgt8 = stack4(wts, 0.0)

    @pl.when(live)
    def _():
        x1_ref[0] = x1
        base_ref[...] = gbase + run_b
        cnt_ref[...] = (gbase + run_b)[:, :128]
        tab_ref[0, 0] = gbase[:, :128].astype(I32)
        tab_ref[0, 1] = run_b[:, :128].astype(I32)
        tab_ref[0, 2] = lstart[:, :128].astype(I32)
        lpos_ref[...] = lpos8
        wgt_ref[...] = wgt8
        h2_sc[...] = h2b
        lpos_sc[...] = lpos8


def _mixer_call(layer, x, mods, g_mix, g_ffn, w_in_b, ln_g, ln_b, w_sp, bs_full, rn_g,
                w_out_b, cos_t, sin_t, dmat, qdec, kdec, cdec, wr_t, br, prev=None):
    bsz, seq, d = x.shape
    n_tok = bsz * seq
    nj = seq // TM
    n_tiles = bsz * nj
    shm, scm, gtm, shf, scf = mods
    tile = lambda t: jnp.minimum(t, n_tiles - 1)
    vec = lambda: pl.BlockSpec((1, 1, d), lambda t: (tile(t) // nj, 0, 0))
    lvec = lambda w: pl.BlockSpec((1, 1, w), lambda t: (layer, 0, 0))
    full = lambda shp: pl.BlockSpec(shp, lambda t: (0,) * len(shp))
    tok = lambda shp: pl.BlockSpec(shp, lambda t: (0, tile(t)))
    in_specs = [
        pl.BlockSpec((1, TM, d), lambda t: (tile(t) // nj, tile(t) % nj, 0)),
        vec(), vec(), vec(), vec(), vec(),
        lvec(d), lvec(d),
        pl.BlockSpec((1, d, 6 * GROUP_W), lambda t: (layer, 0, 0)),
        lvec(GROUP_W), lvec(GROUP_W),
        pl.BlockSpec((1, N_HEADS, CHUNK, CHUNK), lambda t: (layer, 0, 0, 0)),
        pl.BlockSpec((1, CHUNK, GROUP_W), lambda t: (layer, 0, 0)),
        lvec(GROUP_W),
        pl.BlockSpec((1, d, d), lambda t: (layer, 0, 0)),
        pl.BlockSpec((TM, HEAD_DIM), lambda t: (tile(t) % nj, 0)),
        pl.BlockSpec((TM, HEAD_DIM), lambda t: (tile(t) % nj, 0)),
        full((N_HEADS, CHUNK, CHUNK)),
        full((CHUNK, GROUP_W)), full((CHUNK, GROUP_W)), full((1, GROUP_W)),
        pl.BlockSpec((1, N_EXPERTS, d), lambda t: (layer, 0, 0)),
        pl.BlockSpec((1, N_EXPERTS, 1), lambda t: (layer, 0, 0)),
    ]
    out_specs = [
        pl.BlockSpec((1, TM, d), lambda t: (tile(t) // nj, tile(t) % nj, 0)),
        pl.BlockSpec((1, LOCAL_ROWS, d), lambda t: (jnp.maximum(t - 1, 0), 0, 0)),
        tok((8, TM)), tok((8, TM)),
        pl.BlockSpec((1, 3, N_EXPERTS, 128), lambda t: (tile(t), 0, 0, 0)),
        pl.BlockSpec((N_EXPERTS, 128), lambda t: (0, 0)),
    ]
    out_shape = [
        jax.ShapeDtypeStruct((bsz, seq, d), F32),
        jax.ShapeDtypeStruct((n_tiles, LOCAL_ROWS, d), F32),
        jax.ShapeDtypeStruct((8, n_tok), I32),
        jax.ShapeDtypeStruct((8, n_tok), F32),
        jax.ShapeDtypeStruct((n_tiles, 3, N_EXPERTS, 128), I32),
        jax.ShapeDtypeStruct((N_EXPERTS, 128), F32),
    ]
    args = (x, shm, scm, gtm, shf, scf, g_mix, g_ffn, w_in_b, ln_g, ln_b, w_sp, bs_full, rn_g,
            w_out_b, cos_t, sin_t, dmat, qdec, kdec, cdec, wr_t, br)
    scratch = [
        pltpu.VMEM((N_HEADS, HEAD_DIM, HEAD_DIM), F32),
        pltpu.VMEM((N_EXPERTS, TM), F32),
        pltpu.VMEM((TM, d), BF16),
        pltpu.VMEM((TM, d), BF16),
        pltpu.VMEM((8, TM), I32),
    ]
    if prev is not None:
        yl, lpos_tok, w_tok, gtf = prev
        nxt = lambda t: jnp.minimum(t + 1, n_tiles - 1)
        in_specs = [
            pl.BlockSpec((1, LOCAL_ROWS, d // 2), lambda t: (tile(t), 0, 0)),
            pl.BlockSpec((TM, TOP_K), lambda t: (tile(t), 0)),
            pl.BlockSpec((TM, TOP_K), lambda t: (tile(t), 0)),
            pl.BlockSpec((TM, TOP_K), lambda t: (nxt(t), 0)),
            pl.BlockSpec((TM, TOP_K), lambda t: (nxt(t), 0)),
            vec(),
        ] + in_specs
        args = (yl, lpos_tok, w_tok, lpos_tok, w_tok, gtf) + args
        scratch.append(pltpu.VMEM((TM, LOCAL_ROWS), BF16))
    return pl.pallas_call(
        functools.partial(_mixer_kernel, fuse_prev=prev is not None, nj=nj, n_tiles=n_tiles),
        grid=(n_tiles + 1,),
        in_specs=in_specs,
        out_specs=out_specs,
        out_shape=out_shape,
        scratch_shapes=scratch,
        compiler_params=pltpu.CompilerParams(
            dimension_semantics=("arbitrary",), vmem_limit_bytes=VMEM_LIMIT),
        name="mixer_router",
    )(*args)


def _ffn_kernel(be_ref, nused_ref, tab_ref, tabn_ref, w1_ref, b1_ref, w2_ref, b2_ref,
                xl_hbm, yl_hbm, xbuf, ybuf, zbuf, w1b, w2b, sem_in, sem_out, zsem):
    i = pl.program_id(0)
    par = i % 2
    n_used = nused_ref[0]
    tile_pieces = LOCAL_ROWS // GROUP
    pad_pieces = PAD_ROWS // GROUP
    n_tiles = yl_hbm.shape[0] // tile_pieces - 1
    d = xbuf.shape[-1]

    @pl.when((i < n_used) & ((i == 0) | (be_ref[i] != be_ref[jnp.maximum(i - 1, 0)])))
    def _():
        w1b[...] = w1_ref[0].astype(BF16)
        w2b[...] = w2_ref[0].astype(BF16)

    def zero_copy(piece0):
        return pltpu.make_async_copy(zbuf, yl_hbm.at[pl.ds(piece0, pad_pieces)], zsem)

    n_zero = n_tiles + tile_pieces // pad_pieces

    def zero_piece0(z):
        return jnp.where(z < n_tiles, z * tile_pieces + TOP_K * TM // GROUP,
                         n_tiles * tile_pieces + (z - n_tiles) * pad_pieces)

    @pl.when(i == 0)
    def _():
        zbuf[...] = jnp.zeros_like(zbuf)

        def body(z, c):
            zero_copy(zero_piece0(z)).start()
            return c
        lax.fori_loop(0, n_zero, body, 0)

    def start_fetch(tab, slot):
        def body(g, c):
            for u in range(COPY_UNROLL):
                p = g * COPY_UNROLL + u
                pltpu.make_async_copy(xl_hbm.at[tab[p]], xbuf.at[slot, p],
                                      sem_in.at[slot]).start(priority=u % 2)
            return c
        lax.fori_loop(0, PIECES // COPY_UNROLL, body, 0)

    def wait_fetch(slot):
        pltpu.make_async_copy(xl_hbm.at[pl.ds(0, PIECES)], xbuf.at[slot], sem_in.at[slot]).wait()

    def wait_store(slot):
        pltpu.make_async_copy(ybuf.at[slot], yl_hbm.at[pl.ds(0, PIECES)], sem_out.at[slot]).wait()

    @pl.when(i == 0)
    def _():
        start_fetch(tab_ref, 0)

    @pl.when(i + 1 < n_used)
    def _():
        start_fetch(tabn_ref, 1 - par)

    @pl.when(i < n_used)
    def _():
        wait_fetch(par)

        @pl.when(i >= 2)
        def _():
            wait_store(par)

        xb = xbuf[par].reshape(BM, d).astype(BF16)
        a = jnp.dot(xb, w1b[...], preferred_element_type=F32) + b1_ref[0]
        a_glu = jnp.minimum(a[:, :EXPERT_FF], SWIGLU_LIMIT)
        a_lin = jnp.clip(a[:, EXPERT_FF:], -SWIGLU_LIMIT, SWIGLU_LIMIT)
        act = a_glu * jax.nn.sigmoid(SWIGLU_ALPHA * a_glu) * (a_lin + 1.0)
        yb = jnp.dot(act.astype(BF16), w2b[...], preferred_element_type=F32) + b2_ref[0]
        ybuf[par] = _pack_bf16_pairs(yb).reshape(PIECES, GROUP, d // 2)

        @pl.when(i == 0)
        def _():
            def body(z, c):
                zero_copy(0).wait()
                return c
            lax.fori_loop(0, n_zero, body, 0)

        def store(g, c):
            for u in range(COPY_UNROLL):
                p = g * COPY_UNROLL + u
                pltpu.make_async_copy(ybuf.at[par, p], yl_hbm.at[tab_ref[PIECES + p]],
                                      sem_out.at[par]).start(priority=u % 2)
            return c
        lax.fori_loop(0, PIECES // COPY_UNROLL, store, 0)

        @pl.when(i == n_used - 1)
        def _():
            wait_store(par)

            @pl.when(i >= 1)
            def _():
                wait_store(1 - par)


def _ffn_call(block_expert, n_used, piece_tab, xl, w1, b1, w2, b2):
    n_tiles, _, d = xl.shape
    n_blocks = piece_tab.shape[0] // TAB
    tile_pieces = LOCAL_ROWS // GROUP
    xl3 = xl.reshape(n_tiles * tile_pieces, GROUP, d)
    yl_shape = (n_tiles + 1, LOCAL_ROWS, d // 2)
    grid_spec = pltpu.PrefetchScalarGridSpec(
        num_scalar_prefetch=2,
        grid=(n_blocks,),
        in_specs=[
            pl.BlockSpec((TAB,), lambda i, be, nu: (i,), memory_space=pltpu.SMEM),
            pl.BlockSpec((TAB,), lambda i, be, nu: (jnp.minimum(i + 1, n_blocks - 1),),
                         memory_space=pltpu.SMEM),
            pl.BlockSpec((1, d, 2 * EXPERT_FF), lambda i, be, nu: (be[i], 0, 0)),
            pl.BlockSpec((1, 1, 2 * EXPERT_FF), lambda i, be, nu: (be[i], 0, 0)),
            pl.BlockSpec((1, EXPERT_FF, d), lambda i, be, nu: (be[i], 0, 0)),
            pl.BlockSpec((1, 1, d), lambda i, be, nu: (be[i], 0, 0)),
            pl.BlockSpec(memory_space=pl.ANY),
        ],
        out_specs=pl.BlockSpec(memory_space=pl.ANY),
        scratch_shapes=[pltpu.VMEM((2, PIECES, GROUP, d), F32),
                        pltpu.VMEM((2, PIECES, GROUP, d // 2), jnp.uint32),
                        pltpu.VMEM((PAD_ROWS // GROUP, GROUP, d // 2), jnp.uint32),
                        pltpu.VMEM((d, 2 * EXPERT_FF), BF16), pltpu.VMEM((EXPERT_FF, d), BF16),
                        pltpu.SemaphoreType.DMA((2,)), pltpu.SemaphoreType.DMA((2,)),
                        pltpu.SemaphoreType.DMA],
    )
    out = pl.pallas_call(
        _ffn_kernel,
        grid_spec=grid_spec,
        out_shape=jax.ShapeDtypeStruct(((n_tiles + 1) * tile_pieces, GROUP, d // 2), jnp.uint32),
        compiler_params=pltpu.CompilerParams(
            dimension_semantics=("arbitrary",), vmem_limit_bytes=VMEM_LIMIT),
        name="moe_ffn",
    )(block_expert, n_used, piece_tab, piece_tab, w1, b1, w2, b2, xl3)
    return out.reshape(yl_shape)


def _combine_kernel(yl_ref, lpos_ref, w_ref, x1_ref, gt_ref, gfin_ref, o_ref):
    acc = jnp.dot(_unpermute_matrix(lpos_ref[...], w_ref[...]), _unpack_bf16_pairs(yl_ref[0]),
                  preferred_element_type=F32)
    out = x1_ref[...] + gt_ref[0] * acc
    o_ref[...] = out * lax.rsqrt(jnp.mean(out * out, axis=-1, keepdims=True) + EPS) * gfin_ref[...]


def _combine_call(yl, lpos_tok, w_tok, x1, gtf, g_final, seq):
    n_tok, d = x1.shape
    per_b = seq // TM
    return pl.pallas_call(
        _combine_kernel,
        grid=(n_tok // TM,),
        in_specs=[
            pl.BlockSpec((1, LOCAL_ROWS, d // 2), lambda i: (i, 0, 0)),
            pl.BlockSpec((TM, TOP_K), lambda i: (i, 0)),
            pl.BlockSpec((TM, TOP_K), lambda i: (i, 0)),
            pl.BlockSpec((TM, d), lambda i: (i, 0)),
            pl.BlockSpec((1, 1, d), lambda i: (i // per_b, 0, 0)),
            pl.BlockSpec((1, d), lambda i: (0, 0)),
        ],
        out_specs=pl.BlockSpec((TM, d), lambda i: (i, 0)),
        out_shape=jax.ShapeDtypeStruct((n_tok, d), F32),
        compiler_params=pltpu.CompilerParams(
            dimension_semantics=("arbitrary",), vmem_limit_bytes=VMEM_LIMIT),
        name="moe_combine",
    )(yl, lpos_tok, w_tok, x1, gtf, g_final)


def _tables(seq):
    half = HEAD_DIM // 2
    inv_freq = ROPE_BASE ** (-jnp.arange(half, dtype=F32) / half)
    ang = jnp.arange(seq, dtype=F32)[:, None] * inv_freq[None, :]
    cos = jnp.cos(ang)
    sin = jnp.sin(ang)
    cos_t = jnp.concatenate([cos, cos], axis=-1)
    sin_t = jnp.concatenate([-sin, sin], axis=-1)
    gamma = 1.0 - jnp.exp2(-5.0 - jnp.arange(N_HEADS, dtype=F32))
    log_g = jnp.log(gamma)
    idx = jnp.arange(CHUNK, dtype=F32)
    diff = idx[:, None] - idx[None, :]
    dmat = jnp.where(diff[None] >= 0,
                     jnp.exp(log_g[:, None, None] * jnp.maximum(diff, 0.0)[None]), 0.0)
    k_decay = jnp.exp(log_g[:, None] * (CHUNK - 1.0 - idx)[None, :])
    q_decay = jnp.exp(log_g[:, None] * (idx + 1.0)[None, :])
    chunk_decay = jnp.exp(log_g * CHUNK)
    widen = lambda t: jnp.repeat(t.T[:, :, None], HEAD_DIM, axis=2).reshape(CHUNK, GROUP_W)
    qdec = widen(q_decay)
    kdec = widen(k_decay)
    cdec = jnp.repeat(chunk_decay[:, None], HEAD_DIM, axis=1).reshape(1, GROUP_W)
    return cos_t, sin_t, dmat, qdec, kdec, cdec


def kernel(x, c, w_ada, b_ada, g_mix, w_in, gmlp_ln_g, gmlp_ln_b, w_spatial, b_spatial,
           ret_norm_g, w_out, g_ffn, w_router, b_router, w1, b1, w2, b2, g_final):
    bsz, seq, d = x.shape
    depth = w_in.shape[0]
    n_tok = bsz * seq
    n_tiles = n_tok // TM
    n_slots = n_tok * TOP_K + n_tiles * N_EXPERTS * GROUP + N_EXPERTS * BM
    assert n_slots % BM == 0
    n_blocks = n_slots // BM

    mod = _ada_call(c, w_ada, b_ada)
    cos_t, sin_t, dmat, qdec, kdec, cdec = _tables(seq)

    w_in_b = w_in.astype(BF16)
    w_out_b = w_out.astype(BF16)
    wr_t = jnp.swapaxes(w_router, 1, 2).astype(BF16)
    br = b_router.reshape(depth, N_EXPERTS, 1)
    w1_r = w1.reshape(depth * N_EXPERTS, d, 2 * EXPERT_FF)
    w2_r = w2.reshape(depth * N_EXPERTS, EXPERT_FF, d)
    b1_r = b1.reshape(depth * N_EXPERTS, 1, 2 * EXPERT_FF)
    b2_r = b2.reshape(depth * N_EXPERTS, 1, d)
    bs_full = jnp.repeat(jnp.swapaxes(b_spatial, 1, 2)[:, :, :, None], HEAD_DIM, axis=3)
    bs_full = bs_full.reshape(depth, CHUNK, GROUP_W)
    row3 = lambda a: a.reshape(depth, 1, a.shape[-1])
    gfin = g_final.reshape(1, d)
    e_ids = jnp.arange(N_EXPERTS, dtype=I32)
    blk = jnp.arange(n_blocks, dtype=I32)

    piece_off = GROUP * jnp.arange(PIECES, dtype=I32)
    tile_row0 = LOCAL_ROWS * jnp.arange(n_tiles, dtype=I32)
    zero_row = LOCAL_ROWS - GROUP
    spare_row0 = n_tiles * LOCAL_ROWS
    prev = None
    for l in range(depth):
        mods = [m.reshape(bsz, 1, d) for m in jnp.split(mod[l], N_MOD, axis=-1)]
        shm, scm, gtm, shf, scf, gtf = mods
        x1, xl, lpos8, wgt8, tab3, cnt = _mixer_call(
            l, x, (shm, scm, gtm, shf, scf), row3(g_mix), row3(g_ffn), w_in_b,
            row3(gmlp_ln_g), row3(gmlp_ln_b), w_spatial, bs_full, row3(ret_norm_g), w_out_b,
            cos_t, sin_t, dmat, qdec, kdec, cdec, wr_t, br, prev=prev)

        cntp = cnt[:, 0].astype(I32)
        padded = ((cntp + BM - 1) // BM) * BM
        pend = jnp.sum(jnp.where(e_ids[None, :] <= e_ids[:, None], padded[None, :], 0), axis=1)
        pstart = pend - padded
        n_used = pend[-1] // BM
        block_e = jnp.minimum(
            jnp.sum((pend[None, :] <= (blk * BM)[:, None]).astype(I32), axis=1), N_EXPERTS - 1)
        gbase, runs, lstart = tab3[:, 0, :, 0], tab3[:, 1, :, 0], tab3[:, 2, :, 0]
        r = (blk * BM - pstart[block_e])[:, None] + piece_off[None, :]
        gb = gbase.T[block_e]
        rn = runs.T[block_e]
        delta = tile_row0[None, :] + lstart.T[block_e] - gb
        in_run = (gb[:, None, :] <= r[:, :, None]) & (r[:, :, None] < (gb + rn)[:, None, :])
        row = r + jnp.sum(jnp.where(in_run, delta[:, None, :], 0), axis=-1)
        live = (r < cntp[block_e][:, None]) & (blk < n_used)[:, None]
        src = jnp.where(live, row, zero_row)
        dst = jnp.where(live, row, spare_row0 + (blk % 2)[:, None] * BM + piece_off[None, :])
        piece_tab = jnp.concatenate(
            [src // GROUP, dst // GROUP, jnp.zeros((n_blocks, TAB - 2 * PIECES), I32)],
            axis=1).reshape(-1)

        yl = _ffn_call(block_e + l * N_EXPERTS, n_used.reshape(1), piece_tab, xl,
                       w1_r, b1_r, w2_r, b2_r)
        x = x1
        prev = (yl, lpos8[:TOP_K].T, wgt8[:TOP_K].T, gtf)
    out = _combine_call(prev[0], prev[1], prev[2], x.reshape(n_tok, d), prev[3], gfin, seq)
    return out.reshape(bsz, seq, d)
```
